```python
import math
import jax, jax.numpy as jnp
from jax import lax
import numpy as np

D_MODEL = 2048
BATCH = 8
SEQ = 2048
DEPTH = 1

HEAD_DIM = 128
N_HEADS_A = 8
D_A = N_HEADS_A * HEAD_DIM
DILATION_PATTERNS = ((128, 1), (512, 4), (2048, 16))
N_BUCKETS = 32
MAX_DISTANCE = 2048
CHUNK = 128
N_GROUPS_B = 8
D_GROUP_B = 128
D_B = N_GROUPS_B * D_GROUP_B
D_FF = 4 * D_MODEL
D_IN = 3 * D_A + 2 * D_B + 2 * D_MODEL
SPLITS = tuple(np.cumsum([D_A, D_A, D_A, D_B, D_B, D_MODEL]).tolist())
ALPHA = (2 * DEPTH) ** 0.25
BETA = (8 * DEPTH) ** -0.25
LN_EPS = 1e-5
NEG_INF = -1e30

kernel_name = "hybrid_dilated_attn_gmlp_block"


def layer_norm(x, gain, bias):
    xf = x.astype(jnp.float32)
    mean = jnp.mean(xf, axis=-1, keepdims=True)
    var = jnp.mean(jnp.square(xf - mean), axis=-1, keepdims=True)
    y = (xf - mean) * lax.rsqrt(var + LN_EPS) * gain.astype(jnp.float32) + bias.astype(jnp.float32)
    return y.astype(x.dtype)


def t5_causal_bucket(n):
    max_exact = N_BUCKETS // 2
    nf = jnp.maximum(n, 1).astype(jnp.float32)
    large = max_exact + (jnp.log(nf / max_exact) / math.log(MAX_DISTANCE / max_exact)
                         * (N_BUCKETS - max_exact)).astype(jnp.int32)
    large = jnp.minimum(large, N_BUCKETS - 1)
    return jnp.where(n < max_exact, n, large)


def dilated_window_attention(q, k, v, rel_bias, window, dilation):
    B, S, H, Dh = q.shape
    nb = window // dilation
    L = S // dilation
    nblk = -(-L // nb)
    Lp = nblk * nb

    def to_sub(t):
        t = t.reshape(B, L, dilation, H, Dh).transpose(0, 2, 1, 3, 4)
        t = jnp.pad(t, ((0, 0), (0, 0), (0, Lp - L), (0, 0), (0, 0)))
        return t.reshape(B, dilation, nblk, nb, H, Dh)

    def with_prev(t):
        prev = jnp.concatenate([jnp.zeros_like(t[:, :, :1]), t[:, :, :-1]], axis=2)
        return jnp.concatenate([prev, t], axis=3)

    qs = to_sub(q)
    kw = with_prev(to_sub(k))
    vw = with_prev(to_sub(v))

    scores = jnp.einsum('brnqhd,brnkhd->brnhqk', qs, kw).astype(jnp.float32) * (HEAD_DIM ** -0.5)

    qi = jnp.arange(nb)[:, None]
    kj = jnp.arange(2 * nb)[None, :]
    steps = nb + qi - kj
    band = (steps >= 0) & (steps <= nb)
    blk = jnp.arange(nblk)[:, None, None]
    key_ok = (blk * nb + kj[None] - nb) >= 0
    mask = band[None] & key_ok
    bucket = t5_causal_bucket(jnp.clip(steps, 0, nb) * dilation)
    bias = jnp.transpose(rel_bias[bucket].astype(jnp.float32), (2, 0, 1))

    scores = jnp.where(mask[None, None, :, None], scores + bias[None, None, None], NEG_INF)
    m = jnp.max(scores, axis=-1, keepdims=True)
    p = jnp.exp(scores - m)
    den = jnp.sum(p, axis=-1, keepdims=True)
    out = jnp.einsum('brnhqk,brnkhd->brnqhd', p, vw.astype(jnp.float32))
    den_t = jnp.transpose(den[..., 0], (0, 1, 2, 4, 3))
    out = out / den_t[..., None]
    lse = jnp.transpose(m[..., 0], (0, 1, 2, 4, 3)) + jnp.log(den_t)

    def from_sub(t):
        rest = t.shape[5:]
        t = t.reshape((B, dilation, Lp, H) + rest)[:, :, :L]
        t = jnp.moveaxis(t, 1, 2)
        return t.reshape((B, S, H) + rest)

    return from_sub(out), from_sub(lse)


def token_mixers(x, w_in, rel_bias, ln_v_gain, ln_v_bias, w_spatial, b_spatial,
                 w_proj_a, w_proj_b, w_out):
    B, S, _ = x.shape
    proj = jnp.einsum('bsd,de->bse', x, w_in)
    q, k, v, u, vb, ga, gb = jnp.split(proj, SPLITS, axis=-1)

    q = q.reshape(B, S, N_HEADS_A, HEAD_DIM)
    k = k.reshape(B, S, N_HEADS_A, HEAD_DIM)
    v = v.reshape(B, S, N_HEADS_A, HEAD_DIM)
    outs, lses = [], []
    for window, dilation in DILATION_PATTERNS:
        o, l = dilated_window_attention(q, k, v, rel_bias, window, dilation)
        outs.append(o)
        lses.append(l)
    mix_w = jax.nn.softmax(jnp.stack(lses, axis=0), axis=0)
    attn = jnp.sum(mix_w[..., None] * jnp.stack(outs, axis=0), axis=0)
    attn = attn.astype(x.dtype).reshape(B, S, D_A)

    u = jax.nn.gelu(u)
    vb = layer_norm(jax.nn.gelu(vb), ln_v_gain, ln_v_bias)
    nc = S // CHUNK
    vr = vb.reshape(B, nc, CHUNK, N_GROUPS_B, D_GROUP_B)
    causal = jnp.tril(jnp.ones((CHUNK, CHUNK), dtype=bool))
    ws = jnp.where(causal[None], w_spatial, 0.0).astype(vr.dtype)
    z = jnp.einsum('gij,bcjgd->bcigd', ws, vr) + jnp.transpose(b_spatial)[None, None, :, :, None]
    gmlp = (u.reshape(B, nc, CHUNK, N_GROUPS_B, D_GROUP_B) * z).reshape(B, S, D_B)

    y_a = jnp.einsum('bse,ed->bsd', attn, w_proj_a)
    y_b = jnp.einsum('bse,ed->bsd', gmlp, w_proj_b)
    merged = jax.nn.sigmoid(ga) * y_a + jax.nn.sigmoid(gb) * y_b
    return jnp.einsum('bsd,de->bse', merged, w_out)


def squared_relu_mlp(h, w_ff1, b_ff1, w_ff2, b_ff2):
    a = jnp.square(jax.nn.relu(jnp.einsum('bsd,df->bsf', h, w_ff1) + b_ff1))
    return jnp.einsum('bsf,fd->bsd', a, w_ff2) + b_ff2


def setup_inputs(seed: int = 0) -> dict:
    key = jax.random.key(seed)
    ks = jax.random.split(key, 20)
    nrm = jax.random.normal
    f32 = jnp.float32
    x = nrm(ks[0], (BATCH, SEQ, D_MODEL), f32)
    col_scale = jnp.concatenate([
        jnp.ones((2 * D_A,), f32),
        BETA * jnp.ones((D_A,), f32),
        jnp.ones((2 * D_B + 2 * D_MODEL,), f32)]) * (D_MODEL ** -0.5)
    w_in = nrm(ks[1], (DEPTH, D_MODEL, D_IN), f32) * col_scale
    rel_bias = 0.1 * nrm(ks[2], (N_BUCKETS, N_HEADS_A), f32)
    ln_v_gain = 1.0 + 0.01 * nrm(ks[3], (DEPTH, D_B), f32)
    ln_v_bias = 0.01 * nrm(ks[4], (DEPTH, D_B), f32)
    w_spatial = nrm(ks[5], (DEPTH, N_GROUPS_B, CHUNK, CHUNK), f32) * (CHUNK ** -0.5)
    b_spatial = 1.0 + 0.01 * nrm(ks[6], (DEPTH, N_GROUPS_B, CHUNK), f32)
    w_proj_a = nrm(ks[7], (DEPTH, D_A, D_MODEL), f32) * (D_A ** -0.5) * BETA
    w_proj_b = nrm(ks[8], (DEPTH, D_B, D_MODEL), f32) * (D_B ** -0.5) * BETA
    w_out = nrm(ks[9], (DEPTH, D_MODEL, D_MODEL), f32) * (D_MODEL ** -0.5) * BETA
    ln1_gain = 1.0 + 0.01 * nrm(ks[10], (DEPTH, D_MODEL), f32)
    ln1_bias = 0.01 * nrm(ks[11], (DEPTH, D_MODEL), f32)
    w_ff1 = nrm(ks[12], (DEPTH, D_MODEL, D_FF), f32) * (D_MODEL ** -0.5) * BETA
    b_ff1 = 0.01 * nrm(ks[13], (DEPTH, D_FF), f32)
    w_ff2 = nrm(ks[14], (DEPTH, D_FF, D_MODEL), f32) * (D_FF ** -0.5) * BETA
    b_ff2 = 0.01 * nrm(ks[15], (DEPTH, D_MODEL), f32)
    ln2_gain = 1.0 + 0.01 * nrm(ks[16], (DEPTH, D_MODEL), f32)
    ln2_bias = 0.01 * nrm(ks[17], (DEPTH, D_MODEL), f32)
    return {"x": x, "w_in": w_in, "rel_bias": rel_bias, "ln_v_gain": ln_v_gain,
            "ln_v_bias": ln_v_bias, "w_spatial": w_spatial, "b_spatial": b_spatial,
            "w_proj_a": w_proj_a, "w_proj_b": w_proj_b, "w_out": w_out,
            "ln1_gain": ln1_gain, "ln1_bias": ln1_bias, "w_ff1": w_ff1, "b_ff1": b_ff1,
            "w_ff2": w_ff2, "b_ff2": b_ff2, "ln2_gain": ln2_gain, "ln2_bias": ln2_bias}


def reference(x, w_in, rel_bias, ln_v_gain, ln_v_bias, w_spatial, b_spatial,
              w_proj_a, w_proj_b, w_out, ln1_gain, ln1_bias, w_ff1, b_ff1,
              w_ff2, b_ff2, ln2_gain, ln2_bias):
    h = x
    for layer in range(DEPTH):
        mix = token_mixers(h, w_in[layer], rel_bias, ln_v_gain[layer], ln_v_bias[layer],
                           w_spatial[layer], b_spatial[layer], w_proj_a[layer],
                           w_proj_b[layer], w_out[layer])
        h = layer_norm(ALPHA * h + mix, ln1_gain[layer], ln1_bias[layer])
        ff = squared_relu_mlp(h, w_ff1[layer], b_ff1[layer], w_ff2[layer], b_ff2[layer])
        h = layer_norm(ALPHA * h + ff, ln2_gain[layer], ln2_bias[layer])
    return h
```

```python
import functools
import math

import jax
import jax.numpy as jnp
import numpy as np
from jax import lax
from jax.experimental import pallas as pl
from jax.experimental.pallas import tpu as pltpu

D_MODEL = 2048
SEQ = 2048
HEAD_DIM = 128
N_HEADS_A = 8
D_A = N_HEADS_A * HEAD_DIM
DILATIONS = (1, 4, 16)
BLOCK = 128
N_BUCKETS = 32
MAX_DISTANCE = 2048
N_GROUPS_B = 8
D_B = N_GROUPS_B * BLOCK
D_FF = 4 * D_MODEL
D_IN = 3 * D_A + 2 * D_B + 2 * D_MODEL
DEPTH = 1
ALPHA = (2 * DEPTH) ** 0.25
LN_EPS = 1e-5
NEG_INF = -1e30

F32 = jnp.float32
BF16 = jnp.bfloat16

VMEM_LIMIT_BYTES = 56 * 1024 * 1024

PROJ_TM = 1024
PROJ_TN = 1024
MERGE_TM = 256
FFN_TM = 512
FFN_TF = 1024

_Q_TILE, _K_TILE, _V_TILE, _U_TILE, _VB_TILE, _GA_TILE, _GB_TILE = 0, 1, 2, 3, 4, 5, 7


def _layer_norm(x, gain, bias):
    mean = jnp.mean(x, axis=-1, keepdims=True)
    xc = x - mean
    var = jnp.mean(xc * xc, axis=-1, keepdims=True)
    return xc * lax.rsqrt(var + LN_EPS) * gain + bias


def _params(semantics):
    return pltpu.CompilerParams(dimension_semantics=semantics,
                                vmem_limit_bytes=VMEM_LIMIT_BYTES)


def _bucket_tiles():
    def bucket(n):
        max_exact = N_BUCKETS // 2
        nf = np.maximum(n, 1).astype(np.float32)
        large = max_exact + (np.log(nf / np.float32(max_exact))
                             / np.float32(math.log(MAX_DISTANCE / max_exact))
                             * np.float32(N_BUCKETS - max_exact)).astype(np.int32)
        large = np.minimum(large, N_BUCKETS - 1)
        return np.where(n < max_exact, n, large)

    qi = np.arange(BLOCK)[:, None]
    kj = np.arange(2 * BLOCK)[None, :]
    steps = BLOCK + qi - kj
    band = (steps >= 0) & (steps <= BLOCK)
    tiles = []
    for dil in DILATIONS:
        t = np.where(band, bucket(np.clip(steps, 0, BLOCK) * dil), -1).astype(np.int32)
        tiles.append(t)
    return np.concatenate([tiles[0], tiles[1], tiles[2][:, BLOCK:]], axis=1)


def _bias_kernel(rb_ref, bucket_ref, out_ref):
    h = pl.program_id(0)
    bk = bucket_ref[...]
    acc = jnp.full(bk.shape, NEG_INF, F32)
    for b in range(N_BUCKETS):
        acc = jnp.where(bk == b, rb_ref[b, h], acc)
    out_ref[0] = acc


def _bias_tiles(rel_bias):
    buckets = jnp.asarray(_bucket_tiles())
    width = buckets.shape[1]
    return pl.pallas_call(
        _bias_kernel,
        grid=(N_HEADS_A,),
        in_specs=[pl.BlockSpec(memory_space=pltpu.SMEM),
                  pl.BlockSpec((BLOCK, width), lambda h: (0, 0))],
        out_specs=pl.BlockSpec((1, BLOCK, width), lambda h: (h, 0, 0)),
        out_shape=jax.ShapeDtypeStruct((N_HEADS_A, BLOCK, width), F32),
        compiler_params=_params(("arbitrary",)),
        name="bias_tiles",
    )(rel_bias, buckets)


def _proj_kernel(x_ref, w_ref, g_ref, b_ref, o_ref):
    j = pl.program_id(1)
    acc = jnp.dot(x_ref[...], w_ref[...], preferred_element_type=F32)

    @pl.when(j == _Q_TILE)
    def _():
        o_ref[...] = (acc * (HEAD_DIM ** -0.5)).astype(o_ref.dtype)

    @pl.when((j == _K_TILE) | (j == _V_TILE))
    def _():
        o_ref[...] = acc.astype(o_ref.dtype)

    @pl.when(j == _U_TILE)
    def _():
        o_ref[...] = jax.nn.gelu(acc).astype(o_ref.dtype)

    @pl.when(j == _VB_TILE)
    def _():
        o_ref[...] = _layer_norm(jax.nn.gelu(acc), g_ref[...], b_ref[...]).astype(o_ref.dtype)

    @pl.when(j >= _GA_TILE)
    def _():
        o_ref[...] = jax.nn.sigmoid(acc).astype(o_ref.dtype)


def _project(x2d, w_in, ln_v_gain, ln_v_bias):
    m = x2d.shape[0]
    return pl.pallas_call(
        _proj_kernel,
        grid=(m // PROJ_TM, D_IN // PROJ_TN),
        in_specs=[pl.BlockSpec((PROJ_TM, D_MODEL), lambda i, j: (i, 0)),
                  pl.BlockSpec((D_MODEL, PROJ_TN), lambda i, j: (0, j)),
                  pl.BlockSpec((1, D_B), lambda i, j: (0, 0)),
                  pl.BlockSpec((1, D_B), lambda i, j: (0, 0))],
        out_specs=pl.BlockSpec((PROJ_TM, PROJ_TN), lambda i, j: (i, j)),
        out_shape=jax.ShapeDtypeStruct((m, D_IN), BF16),
        compiler_params=_params(("parallel", "arbitrary")),
        name="in_proj",
    )(x2d, w_in, ln_v_gain, ln_v_bias)


def _softmax_block(qb, kw, vw, bias):
    s = lax.dot_general(qb, kw, (((1,), (1,)), ((), ())), preferred_element_type=F32) + bias
    m = jnp.max(s, axis=-1, keepdims=True)
    p = jnp.exp(s - m)
    den = jnp.sum(p, axis=-1, keepdims=True)
    out = jnp.dot(p.astype(BF16), vw, preferred_element_type=F32) / den
    lse = m + jnp.log(den)
    return out, jnp.broadcast_to(lse, out.shape)


def _attn_kernel(q_ref, k_ref, v_ref, bias_ref, o_ref,
                 q32, k32, v32, qp, kp, vp, osc, lsc):
    win = 2 * BLOCK

    def first_block(qs, ks, vs, row, bias_col):
        qb = qs[pl.ds(row, BLOCK), :]
        kw = ks[pl.ds(row, BLOCK), :]
        vw = vs[pl.ds(row, BLOCK), :]
        return _softmax_block(qb, kw, vw, bias_ref[0, :, bias_col:bias_col + BLOCK])

    def full_block(qs, ks, vs, row, bias_col):
        qb = qs[pl.ds(row, BLOCK), :]
        kw = ks[pl.ds(row - BLOCK, win), :]
        vw = vs[pl.ds(row - BLOCK, win), :]
        return _softmax_block(qb, kw, vw, bias_ref[0, :, bias_col:bias_col + win])

    def put(pat, start, stride, out, lse):
        if stride == 1:
            idx = pl.ds(start, BLOCK)
        else:
            idx = pl.ds(start, BLOCK, stride=stride)
        osc[pat, idx, :] = out
        lsc[pat, idx, :] = lse

    out, lse = first_block(q_ref, k_ref, v_ref, 0, BLOCK)
    put(0, 0, 1, out, lse)

    def p0_group(g, carry):
        for j in range(5):
            row = pl.multiple_of((1 + 5 * g + j) * BLOCK, BLOCK)
            out, lse = full_block(q_ref, k_ref, v_ref, row, 0)
            put(0, row, 1, out, lse)
        return carry
    lax.fori_loop(0, 3, p0_group, 0)

    q32[...] = q_ref[...].astype(F32)
    k32[...] = k_ref[...].astype(F32)
    v32[...] = v_ref[...].astype(F32)

    def permute(dil):
        sub = SEQ // dil
        for r in range(dil):
            rows = pl.ds(r, sub, stride=dil)
            qp[r * sub:(r + 1) * sub, :] = q32[rows, :].astype(BF16)
            kp[r * sub:(r + 1) * sub, :] = k32[rows, :].astype(BF16)
            vp[r * sub:(r + 1) * sub, :] = v32[rows, :].astype(BF16)

    permute(4)
    sub1 = SEQ // 4
    for c in range(4):
        out, lse = first_block(qp, kp, vp, c * sub1, 2 * win - BLOCK)
        put(1, c, 4, out, lse)

    def p1_group(c, carry):
        for n in range(1, 4):
            row = pl.multiple_of(c * sub1 + n * BLOCK, BLOCK)
            out, lse = full_block(qp, kp, vp, row, win)
            put(1, c + 4 * BLOCK * n, 4, out, lse)
        return carry
    lax.fori_loop(0, 4, p1_group, 0)

    permute(16)

    def p2_group(g, carry):
        for j in range(4):
            r = 4 * g + j
            row = pl.multiple_of(r * BLOCK, BLOCK)
            out, lse = first_block(qp, kp, vp, row, 2 * win)
            put(2, r, 16, out, lse)
        return carry
    lax.fori_loop(0, 4, p2_group, 0)

    def combine(i, carry):
        rows = pl.ds(pl.multiple_of(i * BLOCK, BLOCK), BLOCK)
        l0, l1, l2 = lsc[0, rows, :], lsc[1, rows, :], lsc[2, rows, :]
        top = jnp.maximum(jnp.maximum(l0, l1), l2)
        w0, w1, w2 = jnp.exp(l0 - top), jnp.exp(l1 - top), jnp.exp(l2 - top)
        mix = w0 * osc[0, rows, :] + w1 * osc[1, rows, :] + w2 * osc[2, rows, :]
        o_ref[rows, :] = (mix / (w0 + w1 + w2)).astype(o_ref.dtype)
        return carry
    lax.fori_loop(0, SEQ // BLOCK, combine, 0)


def _attention(proj, bias_tiles, batch):
    width = bias_tiles.shape[-1]
    qkv_spec = lambda off: pl.BlockSpec((SEQ, HEAD_DIM), lambda b, h: (b, off + h))
    return pl.pallas_call(
        _attn_kernel,
        grid=(batch, N_HEADS_A),
        in_specs=[qkv_spec(0), qkv_spec(N_HEADS_A), qkv_spec(2 * N_HEADS_A),
                  pl.BlockSpec((1, BLOCK, width), lambda b, h: (h, 0, 0))],
        out_specs=pl.BlockSpec((SEQ, HEAD_DIM), lambda b, h: (b, h)),
        out_shape=jax.ShapeDtypeStruct((batch * SEQ, D_A), BF16),
        scratch_shapes=[pltpu.VMEM((SEQ, HEAD_DIM), F32)] * 3
                       + [pltpu.VMEM((SEQ, HEAD_DIM), BF16)] * 3
                       + [pltpu.VMEM((3, SEQ, HEAD_DIM), F32)] * 2,
        compiler_params=_params(("parallel", "arbitrary")),
        name="dilated_attn",
    )(proj, proj, proj, bias_tiles)


def _merge_kernel(attn_ref, u_ref, vb_ref, ga0_ref, ga1_ref, gb0_ref, gb1_ref, x_ref,
                  ws_ref, bs_ref, wpa_ref, wpb_ref, wout_ref, g1_ref, b1_ref,
                  h_ref, gm):
    tm = attn_ref.shape[0]
    row = lax.broadcasted_iota(jnp.int32, (BLOCK, BLOCK), 0)
    col = lax.broadcasted_iota(jnp.int32, (BLOCK, BLOCK), 1)
    causal = col <= row
    for g in range(N_GROUPS_B):
        ws = jnp.where(causal, ws_ref[g], 0.0).astype(BF16)
        cols = slice(g * BLOCK, (g + 1) * BLOCK)
        for c in range(tm // BLOCK):
            rows = slice(c * BLOCK, (c + 1) * BLOCK)
            z = jnp.dot(ws, vb_ref[rows, cols], preferred_element_type=F32) + bs_ref[g]
            gm[rows, cols] = (u_ref[rows, cols].astype(F32) * z).astype(BF16)

    y_a = jnp.dot(attn_ref[...], wpa_ref[...], preferred_element_type=F32)
    y_b = jnp.dot(gm[...], wpb_ref[...], preferred_element_type=F32)
    half = D_MODEL // 2
    merged = []
    for part, (ga_ref, gb_ref) in enumerate(((ga0_ref, gb0_ref), (ga1_ref, gb1_ref))):
        cols = slice(part * half, (part + 1) * half)
        merged.append((ga_ref[...].astype(F32) * y_a[:, cols]
                       + gb_ref[...].astype(F32) * y_b[:, cols]).astype(BF16))
    mix = (jnp.dot(merged[0], wout_ref[:half, :], preferred_element_type=F32)
           + jnp.dot(merged[1], wout_ref[half:, :], preferred_element_type=F32))
    h_ref[...] = _layer_norm(ALPHA * x_ref[...] + mix, g1_ref[...], b1_ref[...])


def _merge(attn, proj, x2d, w_spatial, b_spatial, w_proj_a, w_proj_b, w_out, ln1_gain, ln1_bias):
    m = x2d.shape[0]
    tm = MERGE_TM
    const = lambda shape: pl.BlockSpec(shape, lambda i: (0,) * len(shape),
                                       pipeline_mode=pl.Buffered(1))
    ptile = lambda t: pl.BlockSpec((tm, PROJ_TN), lambda i: (i, t))
    return pl.pallas_call(
        _merge_kernel,
        grid=(m // tm,),
        in_specs=[pl.BlockSpec((tm, D_A), lambda i: (i, 0)),
                  ptile(_U_TILE), ptile(_VB_TILE),
                  ptile(_GA_TILE), ptile(_GA_TILE + 1), ptile(_GB_TILE), ptile(_GB_TILE + 1),
                  pl.BlockSpec((tm, D_MODEL), lambda i: (i, 0)),
                  const((N_GROUPS_B, BLOCK, BLOCK)),
                  const((N_GROUPS_B, BLOCK, 1)),
                  const((D_A, D_MODEL)), const((D_B, D_MODEL)), const((D_MODEL, D_MODEL)),
                  const((1, D_MODEL)), const((1, D_MODEL))],
        out_specs=pl.BlockSpec((tm, D_MODEL), lambda i: (i, 0)),
        out_shape=jax.ShapeDtypeStruct((m, D_MODEL), F32),
        scratch_shapes=[pltpu.VMEM((tm, D_B), BF16)],
        compiler_params=_params(("parallel",)),
        name="merge_ln1",
    )(attn, proj, proj, proj, proj, proj, proj, x2d,
      w_spatial, b_spatial, w_proj_a, w_proj_b, w_out, ln1_gain, ln1_bias)


def _ffn_kernel(h_ref, w1_ref, b1_ref, w2_ref, b2_ref, g2_ref, be2_ref, o_ref, hb):
    f = pl.program_id(1)

    @pl.when(f == 0)
    def _():
        hb[...] = h_ref[...].astype(BF16)

    a = jnp.dot(hb[...], w1_ref[...], preferred_element_type=F32) + b1_ref[...]
    a = jnp.square(jnp.maximum(a, 0.0)).astype(BF16)
    part = jnp.dot(a, w2_ref[...], preferred_element_type=F32)

    @pl.when(f == 0)
    def _():
        o_ref[...] = part

    @pl.when(f > 0)
    def _():
        o_ref[...] += part

    @pl.when(f == pl.num_programs(1) - 1)
    def _():
        o_ref[...] = _layer_norm(ALPHA * h_ref[...] + o_ref[...] + b2_ref[...],
                                 g2_ref[...], be2_ref[...])


def _ffn(h, w_ff1, b_ff1, w_ff2, b_ff2, ln2_gain, ln2_bias):
    m = h.shape[0]
    return pl.pallas_call(
        _ffn_kernel,
        grid=(m // FFN_TM, D_FF // FFN_TF),
        in_specs=[pl.BlockSpec((FFN_TM, D_MODEL), lambda i, f: (i, 0)),
                  pl.BlockSpec((D_MODEL, FFN_TF), lambda i, f: (0, f)),
                  pl.BlockSpec((1, FFN_TF), lambda i, f: (0, f)),
                  pl.BlockSpec((FFN_TF, D_MODEL), lambda i, f: (f, 0)),
                  pl.BlockSpec((1, D_MODEL), lambda i, f: (0, 0)),
                  pl.BlockSpec((1, D_MODEL), lambda i, f: (0, 0)),
                  pl.BlockSpec((1, D_MODEL), lambda i, f: (0, 0))],
        out_specs=pl.BlockSpec((FFN_TM, D_MODEL), lambda i, f: (i, 0)),
        out_shape=jax.ShapeDtypeStruct((m, D_MODEL), F32),
        scratch_shapes=[pltpu.VMEM((FFN_TM, D_MODEL), BF16)],
        compiler_params=_params(("parallel", "arbitrary")),
        name="ffn_ln2",
    )(h, w_ff1, b_ff1, w_ff2, b_ff2, ln2_gain, ln2_bias)


def kernel(x, w_in, rel_bias, ln_v_gain, ln_v_bias, w_spatial, b_spatial, w_proj_a, w_proj_b,
           w_out, ln1_gain, ln1_bias, w_ff1, b_ff1, w_ff2, b_ff2, ln2_gain, ln2_bias):
    batch, seq, d_model = x.shape
    assert (seq, d_model) == (SEQ, D_MODEL) and w_in.shape[0] == DEPTH
    bias_tiles = _bias_tiles(rel_bias)
    h = x.reshape(batch * seq, d_model)
    for layer in range(DEPTH):
        proj = _project(h.astype(BF16), w_in[layer].astype(BF16),
                        ln_v_gain[layer][None, :], ln_v_bias[layer][None, :])
        attn = _attention(proj, bias_tiles, batch)
        h = _merge(attn, proj, h, w_spatial[layer], b_spatial[layer][:, :, None],
                   w_proj_a[layer].astype(BF16), w_proj_b[layer].astype(BF16),
                   w_out[layer].astype(BF16), ln1_gain[layer][None, :], ln1_bias[layer][None, :])
        h = _ffn(h, w_ff1[layer].astype(BF16), b_ff1[layer][None, :],
                 w_ff2[layer].astype(BF16), b_ff2[layer][None, :],
                 ln2_gain[layer][None, :], ln2_bias[layer][None, :])
    return h.reshape(batch, seq, d_model)
```

```python
import math

import jax
import jax.numpy as jnp
import numpy as np
from jax import lax
from jax.experimental import pallas as pl
from jax.experimental.pallas import tpu as pltpu

D_MODEL = 2048
SEQ = 2048
HEAD_DIM = 128
N_HEADS_A = 8
D_A = N_HEADS_A * HEAD_DIM
DILATIONS = (1, 4, 16)
BLOCK = 128
N_BUCKETS = 32
MAX_DISTANCE = 2048
N_GROUPS_B = 8
D_B = N_GROUPS_B * BLOCK
D_FF = 4 * D_MODEL
D_QKV = 3 * D_A
D_REST = 2 * D_B + 2 * D_MODEL
DEPTH = 1
ALPHA = (2 * DEPTH) ** 0.25
LN_EPS = 1e-5
NEG_INF = -1e30

F32 = jnp.float32
BF16 = jnp.bfloat16

VMEM_LIMIT_BYTES = 56 * 1024 * 1024

PROJ_TM = 1024
PROJ_TN = 1024
MERGE_TM = 256
FFN_TM = 512
FFN_TF = 1024

_Q_TILE, _V_TILE, _U_TILE, _VB_TILE, _GA_TILE = 0, 2, 3, 4, 5
_N_QKV_TILES = D_QKV // PROJ_TN


def _layer_norm(x, gain, bias):
    mean = jnp.mean(x, axis=-1, keepdims=True)
    xc = x - mean
    var = jnp.mean(xc * xc, axis=-1, keepdims=True)
    return xc * lax.rsqrt(var + LN_EPS) * gain + bias


def _params(semantics):
    return pltpu.CompilerParams(dimension_semantics=semantics,
                                vmem_limit_bytes=VMEM_LIMIT_BYTES)


def _bucket_tiles():
    def bucket(n):
        max_exact = N_BUCKETS // 2
        nf = np.maximum(n, 1).astype(np.float32)
        large = max_exact + (np.log(nf / np.float32(max_exact))
                             / np.float32(math.log(MAX_DISTANCE / max_exact))
                             * np.float32(N_BUCKETS - max_exact)).astype(np.int32)
        large = np.minimum(large, N_BUCKETS - 1)
        return np.where(n < max_exact, n, large)

    qi = np.arange(BLOCK)[:, None]
    kj = np.arange(2 * BLOCK)[None, :]
    steps = BLOCK + qi - kj
    band = (steps >= 0) & (steps <= BLOCK)
    tiles = []
    for dil in DILATIONS:
        t = np.where(band, bucket(np.clip(steps, 0, BLOCK) * dil), -1).astype(np.int32)
        tiles.append(t)
    return np.concatenate(tiles, axis=1)


def _bias_kernel(rb_ref, bucket_ref, out_ref):
    h = pl.program_id(0)
    bk = bucket_ref[...]
    acc = jnp.full(bk.shape, NEG_INF, F32)
    for b in range(N_BUCKETS):
        acc = jnp.where(bk == b, rb_ref[b, h], acc)
    out_ref[0] = acc


def _bias_tiles(rel_bias):
    buckets = jnp.asarray(_bucket_tiles())
    width = buckets.shape[1]
    return pl.pallas_call(
        _bias_kernel,
        grid=(N_HEADS_A,),
        in_specs=[pl.BlockSpec(memory_space=pltpu.SMEM),
                  pl.BlockSpec((BLOCK, width), lambda h: (0, 0))],
        out_specs=pl.BlockSpec((1, BLOCK, width), lambda h: (h, 0, 0)),
        out_shape=jax.ShapeDtypeStruct((N_HEADS_A, BLOCK, width), F32),
        compiler_params=_params(("arbitrary",)),
        name="bias_tiles",
    )(rel_bias, buckets)


def _proj_kernel(x_ref, w_ref, g_ref, b_ref, qkv_ref, rest_ref, xb):
    j = pl.program_id(1)

    @pl.when(j == 0)
    def _():
        xb[...] = x_ref[...].astype(BF16)

    acc = jnp.dot(xb[...], w_ref[...], preferred_element_type=F32)

    @pl.when(j == _Q_TILE)
    def _():
        qkv_ref[...] = acc * (HEAD_DIM ** -0.5)

    @pl.when((j > _Q_TILE) & (j <= _V_TILE))
    def _():
        qkv_ref[...] = acc

    @pl.when(j == _U_TILE)
    def _():
        rest_ref[...] = jax.nn.gelu(acc).astype(BF16)

    @pl.when(j == _VB_TILE)
    def _():
        rest_ref[...] = _layer_norm(jax.nn.gelu(acc), g_ref[...], b_ref[...]).astype(BF16)

    @pl.when(j >= _GA_TILE)
    def _():
        rest_ref[...] = jax.nn.sigmoid(acc).astype(BF16)


def _project(x2d, w_in, ln_v_gain, ln_v_bias):
    m = x2d.shape[0]
    n_tiles = (D_QKV + D_REST) // PROJ_TN
    last_qkv = _N_QKV_TILES - 1
    return pl.pallas_call(
        _proj_kernel,
        grid=(m // PROJ_TM, n_tiles),
        in_specs=[pl.BlockSpec((PROJ_TM, D_MODEL), lambda i, j: (i, 0)),
                  pl.BlockSpec((D_MODEL, PROJ_TN), lambda i, j: (0, j)),
                  pl.BlockSpec((1, D_B), lambda i, j: (0, 0)),
                  pl.BlockSpec((1, D_B), lambda i, j: (0, 0))],
        out_specs=[pl.BlockSpec((PROJ_TM, PROJ_TN), lambda i, j: (i, jnp.minimum(j, last_qkv))),
                   pl.BlockSpec((PROJ_TM, PROJ_TN),
                                lambda i, j: (i, jnp.maximum(j - _N_QKV_TILES, 0)))],
        out_shape=[jax.ShapeDtypeStruct((m, D_QKV), F32),
                   jax.ShapeDtypeStruct((m, D_REST), BF16)],
        scratch_shapes=[pltpu.VMEM((PROJ_TM, D_MODEL), BF16)],
        compiler_params=_params(("parallel", "arbitrary")),
        name="in_proj",
    )(x2d, w_in, ln_v_gain, ln_v_bias)


def _rows(start, size, stride):
    return pl.ds(start, size) if stride == 1 else pl.ds(start, size, stride=stride)


def _softmax_block(qb, kw, vw, bias):
    s = lax.dot_general(qb, kw, (((1,), (1,)), ((), ())), preferred_element_type=F32) + bias
    m = jnp.max(s, axis=-1, keepdims=True)
    p = jnp.exp(s - m)
    den = jnp.sum(p, axis=-1, keepdims=True)
    num = jnp.dot(p.astype(BF16), vw, preferred_element_type=F32)
    return num, jnp.broadcast_to(m, num.shape), jnp.broadcast_to(den, num.shape)


def _attn_kernel(q_ref, k_ref, v_ref, bias_ref, o_ref, qf4, kf4, vf4, nsc, msc, dsc):
    win = 2 * BLOCK
    sub = SEQ // 4
    n_blk = sub // BLOCK

    def block(refs, q_start, stride, first, bias_col):
        qs, ks, vs = refs
        k_start = q_start if first else q_start - stride * BLOCK
        width = BLOCK if first else win
        qb = qs[_rows(q_start, BLOCK, stride), :].astype(BF16)
        kw = ks[_rows(k_start, width, stride), :].astype(BF16)
        vw = vs[_rows(k_start, width, stride), :].astype(BF16)
        col = bias_col + (BLOCK if first else 0)
        return _softmax_block(qb, kw, vw, bias_ref[0, :, col:col + width])

    def put(pat, rows, res):
        nsc[pat, rows, :], msc[pat, rows, :], dsc[pat, rows, :] = res

    nat = (q_ref, k_ref, v_ref)
    for n in range(SEQ // BLOCK):
        put(0, _rows(n * BLOCK, BLOCK, 1), block(nat, n * BLOCK, 1, n == 0, 0))

    for c in range(4):
        for n in range(n_blk):
            put(1, _rows(c * sub + n * BLOCK, BLOCK, 1),
                block(nat, c + 4 * n * BLOCK, 4, n == 0, win))

    for c in range(4):
        dst = _rows(c * sub, sub, 1)
        src = _rows(c, sub, 4)
        qf4[dst, :] = q_ref[src, :]
        kf4[dst, :] = k_ref[src, :]
        vf4[dst, :] = v_ref[src, :]

    mod4 = (qf4, kf4, vf4)
    for c in range(4):
        for r in range(4):
            start = c * sub + r
            put(2, _rows(start, BLOCK, 4), block(mod4, start, 4, True, 2 * win))

    for c in range(4):
        for n in range(n_blk):
            tok = _rows(c + 4 * n * BLOCK, BLOCK, 4)
            m4 = _rows(c * sub + n * BLOCK, BLOCK, 1)
            m0, m1, m2 = msc[0, tok, :], msc[1, m4, :], msc[2, m4, :]
            top = jnp.maximum(jnp.maximum(m0, m1), m2)
            w0, w1, w2 = jnp.exp(m0 - top), jnp.exp(m1 - top), jnp.exp(m2 - top)
            num = w0 * nsc[0, tok, :] + w1 * nsc[1, m4, :] + w2 * nsc[2, m4, :]
            den = w0 * dsc[0, tok, :] + w1 * dsc[1, m4, :] + w2 * dsc[2, m4, :]
            o_ref[tok, :] = num / den


def _attention(qkv, bias_tiles, batch):
    width = bias_tiles.shape[-1]
    qkv_spec = lambda off: pl.BlockSpec((SEQ, HEAD_DIM), lambda b, h: (b, off + h))
    return pl.pallas_call(
        _attn_kernel,
        grid=(batch, N_HEADS_A),
        in_specs=[qkv_spec(0), qkv_spec(N_HEADS_A), qkv_spec(2 * N_HEADS_A),
                  pl.BlockSpec((1, BLOCK, width), lambda b, h: (h, 0, 0))],
        out_specs=pl.BlockSpec((SEQ, HEAD_DIM), lambda b, h: (b, h)),
        out_shape=jax.ShapeDtypeStruct((batch * SEQ, D_A), F32),
        scratch_shapes=[pltpu.VMEM((SEQ, HEAD_DIM), F32)] * 3
                       + [pltpu.VMEM((3, SEQ, HEAD_DIM), F32)] * 3,
        compiler_params=_params(("parallel", "arbitrary")),
        name="dilated_attn",
    )(qkv, qkv, qkv, bias_tiles)


def _merge_kernel(attn_ref, u_ref, vb_ref, ga_ref, gb_ref, x_ref,
                  ws_ref, bs_ref, wpa_ref, wpb_ref, wout_ref, g1_ref, b1_ref,
                  h_ref, gm):
    tm = attn_ref.shape[0]
    row = lax.broadcasted_iota(jnp.int32, (BLOCK, BLOCK), 0)
    col = lax.broadcasted_iota(jnp.int32, (BLOCK, BLOCK), 1)
    causal = col <= row
    for g in range(N_GROUPS_B):
        ws = jnp.where(causal, ws_ref[g], 0.0).astype(BF16)
        cols = slice(g * BLOCK, (g + 1) * BLOCK)
        for c in range(tm // BLOCK):
            rows = slice(c * BLOCK, (c + 1) * BLOCK)
            z = jnp.dot(ws, vb_ref[rows, cols], preferred_element_type=F32) + bs_ref[g]
            gm[rows, cols] = (u_ref[rows, cols].astype(F32) * z).astype(BF16)

    y_a = jnp.dot(attn_ref[...].astype(BF16), wpa_ref[...], preferred_element_type=F32)
    y_b = jnp.dot(gm[...], wpb_ref[...], preferred_element_type=F32)
    merged = (ga_ref[...].astype(F32) * y_a + gb_ref[...].astype(F32) * y_b).astype(BF16)
    mix = jnp.dot(merged, wout_ref[...], preferred_element_type=F32)
    h_ref[...] = _layer_norm(ALPHA * x_ref[...] + mix, g1_ref[...], b1_ref[...])


def _merge(attn, rest, x2d, w_spatial, b_spatial, w_proj_a, w_proj_b, w_out, ln1_gain, ln1_bias):
    m = x2d.shape[0]
    tm = MERGE_TM
    const = lambda shape: pl.BlockSpec(shape, lambda i: (0,) * len(shape),
                                       pipeline_mode=pl.Buffered(1))
    return pl.pallas_call(
        _merge_kernel,
        grid=(m // tm,),
        in_specs=[pl.BlockSpec((tm, D_A), lambda i: (i, 0)),
                  pl.BlockSpec((tm, D_B), lambda i: (i, 0)),
                  pl.BlockSpec((tm, D_B), lambda i: (i, 1)),
                  pl.BlockSpec((tm, D_MODEL), lambda i: (i, 1)),
                  pl.BlockSpec((tm, D_MODEL), lambda i: (i, 2)),
                  pl.BlockSpec((tm, D_MODEL), lambda i: (i, 0)),
                  const((N_GROUPS_B, BLOCK, BLOCK)),
                  const((N_GROUPS_B, BLOCK, 1)),
                  const((D_A, D_MODEL)), const((D_B, D_MODEL)), const((D_MODEL, D_MODEL)),
                  const((1, D_MODEL)), const((1, D_MODEL))],
        out_specs=pl.BlockSpec((tm, D_MODEL), lambda i: (i, 0)),
        out_shape=jax.ShapeDtypeStruct((m, D_MODEL), F32),
        scratch_shapes=[pltpu.VMEM((tm, D_B), BF16)],
        compiler_params=_params(("parallel",)),
        name="merge_ln1",
    )(attn, rest, rest, rest, rest, x2d,
      w_spatial, b_spatial, w_proj_a, w_proj_b, w_out, ln1_gain, ln1_bias)


def _ffn_kernel(h_ref, w1_ref, b1_ref, w2_ref, b2_ref, g2_ref, be2_ref, o_ref, hb):
    f = pl.program_id(1)

    @pl.when(f == 0)
    def _():
        hb[...] = h_ref[...].astype(BF16)

    a = jnp.dot(hb[...], w1_ref[...], preferred_element_type=F32) + b1_ref[...]
    a = jnp.square(jnp.maximum(a, 0.0)).astype(BF16)
    part = jnp.dot(a, w2_ref[...], preferred_element_type=F32)

    @pl.when(f == 0)
    def _():
        o_ref[...] = part

    @pl.when(f > 0)
    def _():
        o_ref[...] += part

    @pl.when(f == pl.num_programs(1) - 1)
    def _():
        o_ref[...] = _layer_norm(ALPHA * h_ref[...] + o_ref[...] + b2_ref[...],
                                 g2_ref[...], be2_ref[...])


def _ffn(h, w_ff1, b_ff1, w_ff2, b_ff2, ln2_gain, ln2_bias):
    m = h.shape[0]
    return pl.pallas_call(
        _ffn_kernel,
        grid=(m // FFN_TM, D_FF // FFN_TF),
        in_specs=[pl.BlockSpec((FFN_TM, D_MODEL), lambda i, f: (i, 0)),
                  pl.BlockSpec((D_MODEL, FFN_TF), lambda i, f: (0, f)),
                  pl.BlockSpec((1, FFN_TF), lambda i, f: (0, f)),
                  pl.BlockSpec((FFN_TF, D_MODEL), lambda i, f: (f, 0)),
                  pl.BlockSpec((1, D_MODEL), lambda i, f: (0, 0)),
                  pl.BlockSpec((1, D_MODEL), lambda i, f: (0, 0)),
                  pl.BlockSpec((1, D_MODEL), lambda i, f: (0, 0))],
        out_specs=pl.BlockSpec((FFN_TM, D_MODEL), lambda i, f: (i, 0)),
        out_shape=jax.ShapeDtypeStruct((m, D_MODEL), F32),
        scratch_shapes=[pltpu.VMEM((FFN_TM, D_MODEL), BF16)],
        compiler_params=_params(("parallel", "arbitrary")),
        name="ffn_ln2",
    )(h, w_ff1, b_ff1, w_ff2, b_ff2, ln2_gain, ln2_bias)


def kernel(x, w_in, rel_bias, ln_v_gain, ln_v_bias, w_spatial, b_spatial, w_proj_a, w_proj_b,
           w_out, ln1_gain, ln1_bias, w_ff1, b_ff1, w_ff2, b_ff2, ln2_gain, ln2_bias):
    batch, seq, d_model = x.shape
    assert (seq, d_model) == (SEQ, D_MODEL) and w_in.shape[0] == DEPTH
    bias_tiles = _bias_tiles(rel_bias)
    h = x.reshape(batch * seq, d_model)
    for layer in range(DEPTH):
        qkv, rest = _project(h, w_in[layer].astype(BF16),
                             ln_v_gain[layer][None, :], ln_v_bias[layer][None, :])
        attn = _attention(qkv, bias_tiles, batch)
        h = _merge(attn, rest, h, w_spatial[layer], b_spatial[layer][:, :, None],
                   w_proj_a[layer].astype(BF16), w_proj_b[layer].astype(BF16),
                   w_out[layer].astype(BF16), ln1_gain[layer][None, :], ln1_bias[layer][None, :])
        h = _ffn(h, w_ff1[layer].astype(BF16), b_ff1[layer][None, :],
                 w_ff2[layer].astype(BF16), b_ff2[layer][None, :],
                 ln2_gain[layer][None, :], ln2_bias[layer][None, :])
    return h.reshape(batch, seq, d_model)
```

```python
import math

import jax
import jax.numpy as jnp
import numpy as np
from jax import lax
from jax.experimental import pallas as pl
from jax.experimental.pallas import tpu as pltpu

D_MODEL = 2048
SEQ = 2048
HEAD_DIM = 128
N_HEADS_A = 8
D_A = N_HEADS_A * HEAD_DIM
DILATIONS = (1, 4, 16)
BLOCK = 128
N_BUCKETS = 32
MAX_DISTANCE = 2048
N_GROUPS_B = 8
D_B = N_GROUPS_B * BLOCK
D_FF = 4 * D_MODEL
D_QKV = 3 * D_A
D_REST = 2 * D_B + 2 * D_MODEL
DEPTH = 1
ALPHA = (2 * DEPTH) ** 0.25
LN_EPS = 1e-5
NEG_INF = -1e30

F32 = jnp.float32
BF16 = jnp.bfloat16

VMEM_LIMIT_BYTES = 56 * 1024 * 1024

PROJ_TM = 1024
PROJ_TN = 1024
MERGE_TM = 256
FFN_TM = 512
FFN_TF = 1024


def _layer_norm(x, gain, bias):
    mean = jnp.mean(x, axis=-1, keepdims=True)
    xc = x - mean
    var = jnp.mean(xc * xc, axis=-1, keepdims=True)
    return xc * lax.rsqrt(var + LN_EPS) * gain + bias


def _params(semantics):
    return pltpu.CompilerParams(dimension_semantics=semantics,
                                vmem_limit_bytes=VMEM_LIMIT_BYTES)


def _bucket_tiles():
    def bucket(n):
        max_exact = N_BUCKETS // 2
        nf = np.maximum(n, 1).astype(np.float32)
        large = max_exact + (np.log(nf / np.float32(max_exact))
                             / np.float32(math.log(MAX_DISTANCE / max_exact))
                             * np.float32(N_BUCKETS - max_exact)).astype(np.int32)
        large = np.minimum(large, N_BUCKETS - 1)
        return np.where(n < max_exact, n, large)

    qi = np.arange(BLOCK)[:, None]
    kj = np.arange(2 * BLOCK)[None, :]
    steps = BLOCK + qi - kj
    band = (steps >= 0) & (steps <= BLOCK)
    tiles = []
    for dil in DILATIONS:
        t = np.where(band, bucket(np.clip(steps, 0, BLOCK) * dil), -1).astype(np.int32)
        tiles.append(t)
    return np.concatenate(tiles, axis=1)


def _bias_kernel(rb_ref, bucket_ref, out_ref):
    h = pl.program_id(0)
    bk = bucket_ref[...]
    acc = jnp.full(bk.shape, NEG_INF, F32)
    for b in range(N_BUCKETS):
        acc = jnp.where(bk == b, rb_ref[b, h], acc)
    out_ref[0] = acc


def _bias_tiles(rel_bias):
    buckets = jnp.asarray(_bucket_tiles())
    width = buckets.shape[1]
    return pl.pallas_call(
        _bias_kernel,
        grid=(N_HEADS_A,),
        in_specs=[pl.BlockSpec(memory_space=pltpu.SMEM),
                  pl.BlockSpec((BLOCK, width), lambda h: (0, 0))],
        out_specs=pl.BlockSpec((1, BLOCK, width), lambda h: (h, 0, 0)),
        out_shape=jax.ShapeDtypeStruct((N_HEADS_A, BLOCK, width), F32),
        compiler_params=_params(("arbitrary",)),
        name="bias_tiles",
    )(rel_bias, buckets)


def _proj_kernel(x_ref, w_ref, s_ref, o_ref, xb):
    @pl.when(pl.program_id(1) == 0)
    def _():
        xb[...] = x_ref[...].astype(BF16)

    acc = jnp.dot(xb[...], w_ref[...], preferred_element_type=F32)
    o_ref[...] = (acc * s_ref[...]).astype(o_ref.dtype)


def _project(x2d, w_in, col_scale, first_col, n_cols, out_dtype, name):
    m = x2d.shape[0]
    first_tile = first_col // PROJ_TN
    return pl.pallas_call(
        _proj_kernel,
        grid=(m // PROJ_TM, n_cols // PROJ_TN),
        in_specs=[pl.BlockSpec((PROJ_TM, D_MODEL), lambda i, j: (i, 0)),
                  pl.BlockSpec((D_MODEL, PROJ_TN), lambda i, j: (0, j + first_tile)),
                  pl.BlockSpec((1, PROJ_TN), lambda i, j: (0, j + first_tile))],
        out_specs=pl.BlockSpec((PROJ_TM, PROJ_TN), lambda i, j: (i, j)),
        out_shape=jax.ShapeDtypeStruct((m, n_cols), out_dtype),
        scratch_shapes=[pltpu.VMEM((PROJ_TM, D_MODEL), BF16)],
        compiler_params=_params(("parallel", "arbitrary")),
        name=name,
    )(x2d, w_in, col_scale)


def _rows(start, size, stride):
    return pl.ds(start, size) if stride == 1 else pl.ds(start, size, stride=stride)


def _softmax_block(qb, kw, vw, bias):
    s = lax.dot_general(qb, kw, (((1,), (1,)), ((), ())), preferred_element_type=F32) + bias
    m = jnp.max(s, axis=-1, keepdims=True)
    p = jnp.exp(s - m)
    den = jnp.sum(p, axis=-1, keepdims=True)
    num = jnp.dot(p.astype(BF16), vw, preferred_element_type=F32)
    return num, jnp.broadcast_to(m, num.shape), jnp.broadcast_to(den, num.shape)


def _attn_kernel(q_ref, k_ref, v_ref, bias_ref, o_ref, qf4, kf4, vf4, nsc, msc, dsc):
    win = 2 * BLOCK
    sub = SEQ // 4
    n_blk = sub // BLOCK

    def block(refs, q_start, stride, first, bias_col):
        qs, ks, vs = refs
        k_start = q_start if first else q_start - stride * BLOCK
        width = BLOCK if first else win
        qb = qs[_rows(q_start, BLOCK, stride), :].astype(BF16)
        kw = ks[_rows(k_start, width, stride), :].astype(BF16)
        vw = vs[_rows(k_start, width, stride), :].astype(BF16)
        col = bias_col + (BLOCK if first else 0)
        return _softmax_block(qb, kw, vw, bias_ref[0, :, col:col + width])

    def put(pat, rows, res):
        nsc[pat, rows, :], msc[pat, rows, :], dsc[pat, rows, :] = res

    nat = (q_ref, k_ref, v_ref)
    for n in range(SEQ // BLOCK):
        put(0, _rows(n * BLOCK, BLOCK, 1), block(nat, n * BLOCK, 1, n == 0, 0))

    for c in range(4):
        for n in range(n_blk):
            put(1, _rows(c * sub + n * BLOCK, BLOCK, 1),
                block(nat, c + 4 * n * BLOCK, 4, n == 0, win))

    for c in range(4):
        dst = _rows(c * sub, sub, 1)
        src = _rows(c, sub, 4)
        qf4[dst, :] = q_ref[src, :]
        kf4[dst, :] = k_ref[src, :]
        vf4[dst, :] = v_ref[src, :]

    mod4 = (qf4, kf4, vf4)
    for c in range(4):
        for r in range(4):
            start = c * sub + r
            put(2, _rows(start, BLOCK, 4), block(mod4, start, 4, True, 2 * win))

    for c in range(4):
        for n in range(n_blk):
            tok = _rows(c + 4 * n * BLOCK, BLOCK, 4)
            m4 = _rows(c * sub + n * BLOCK, BLOCK, 1)
            m0, m1, m2 = msc[0, tok, :], msc[1, m4, :], msc[2, m4, :]
            top = jnp.maximum(jnp.maximum(m0, m1), m2)
            w0, w1, w2 = jnp.exp(m0 - top), jnp.exp(m1 - top), jnp.exp(m2 - top)
            num = w0 * nsc[0, tok, :] + w1 * nsc[1, m4, :] + w2 * nsc[2, m4, :]
            den = w0 * dsc[0, tok, :] + w1 * dsc[1, m4, :] + w2 * dsc[2, m4, :]
            o_ref[tok, :] = num / den


def _attention(qkv, bias_tiles, batch):
    width = bias_tiles.shape[-1]
    qkv_spec = lambda off: pl.BlockSpec((SEQ, HEAD_DIM), lambda b, h: (b, off + h))
    return pl.pallas_call(
        _attn_kernel,
        grid=(batch, N_HEADS_A),
        in_specs=[qkv_spec(0), qkv_spec(N_HEADS_A), qkv_spec(2 * N_HEADS_A),
                  pl.BlockSpec((1, BLOCK, width), lambda b, h: (h, 0, 0))],
        out_specs=pl.BlockSpec((SEQ, HEAD_DIM), lambda b, h: (b, h)),
        out_shape=jax.ShapeDtypeStruct((batch * SEQ, D_A), F32),
        scratch_shapes=[pltpu.VMEM((SEQ, HEAD_DIM), F32)] * 3
                       + [pltpu.VMEM((3, SEQ, HEAD_DIM), F32)] * 3,
        compiler_params=_params(("parallel", "arbitrary")),
        name="dilated_attn",
    )(qkv, qkv, qkv, bias_tiles)


def _merge_kernel(attn_ref, u_ref, vb_ref, ga_ref, gb_ref, x_ref, gv_ref, bv_ref,
                  ws_ref, bs_ref, wpa_ref, wpb_ref, wout_ref, g1_ref, b1_ref,
                  h_ref, gm):
    tm = attn_ref.shape[0]
    row = lax.broadcasted_iota(jnp.int32, (BLOCK, BLOCK), 0)
    col = lax.broadcasted_iota(jnp.int32, (BLOCK, BLOCK), 1)
    causal = col <= row
    vbn = _layer_norm(jax.nn.gelu(vb_ref[...].astype(F32)), gv_ref[...], bv_ref[...]).astype(BF16)
    for g in range(N_GROUPS_B):
        ws = jnp.where(causal, ws_ref[g], 0.0).astype(BF16)
        cols = slice(g * BLOCK, (g + 1) * BLOCK)
        for c in range(tm // BLOCK):
            rows = slice(c * BLOCK, (c + 1) * BLOCK)
            z = jnp.dot(ws, vbn[rows, cols], preferred_element_type=F32) + bs_ref[g]
            u = jax.nn.gelu(u_ref[rows, cols].astype(F32))
            gm[rows, cols] = (u * z).astype(BF16)

    y_a = jnp.dot(attn_ref[...].astype(BF16), wpa_ref[...], preferred_element_type=F32)
    y_b = jnp.dot(gm[...], wpb_ref[...], preferred_element_type=F32)
    merged = (jax.nn.sigmoid(ga_ref[...].astype(F32)) * y_a
              + jax.nn.sigmoid(gb_ref[...].astype(F32)) * y_b).astype(BF16)
    mix = jnp.dot(merged, wout_ref[...], preferred_element_type=F32)
    h_ref[...] = _layer_norm(ALPHA * x_ref[...] + mix, g1_ref[...], b1_ref[...])


def _merge(attn, rest, x2d, ln_v_gain, ln_v_bias, w_spatial, b_spatial, w_proj_a, w_proj_b, w_out,
           ln1_gain, ln1_bias):
    m = x2d.shape[0]
    tm = MERGE_TM
    const = lambda shape: pl.BlockSpec(shape, lambda i: (0,) * len(shape),
                                       pipeline_mode=pl.Buffered(1))
    return pl.pallas_call(
        _merge_kernel,
        grid=(m // tm,),
        in_specs=[pl.BlockSpec((tm, D_A), lambda i: (i, 0)),
                  pl.BlockSpec((tm, D_B), lambda i: (i, 0)),
                  pl.BlockSpec((tm, D_B), lambda i: (i, 1)),
                  pl.BlockSpec((tm, D_MODEL), lambda i: (i, 1)),
                  pl.BlockSpec((tm, D_MODEL), lambda i: (i, 2)),
                  pl.BlockSpec((tm, D_MODEL), lambda i: (i, 0)),
                  const((1, D_B)), const((1, D_B)),
                  const((N_GROUPS_B, BLOCK, BLOCK)),
                  const((N_GROUPS_B, BLOCK, 1)),
                  const((D_A, D_MODEL)), const((D_B, D_MODEL)), const((D_MODEL, D_MODEL)),
                  const((1, D_MODEL)), const((1, D_MODEL))],
        out_specs=pl.BlockSpec((tm, D_MODEL), lambda i: (i, 0)),
        out_shape=jax.ShapeDtypeStruct((m, D_MODEL), F32),
        scratch_shapes=[pltpu.VMEM((tm, D_B), BF16)],
        compiler_params=_params(("parallel",)),
        name="merge_ln1",
    )(attn, rest, rest, rest, rest, x2d, ln_v_gain, ln_v_bias,
      w_spatial, b_spatial, w_proj_a, w_proj_b, w_out, ln1_gain, ln1_bias)


def _ffn_kernel(h_ref, w1_ref, b1_ref, w2_ref, b2_ref, g2_ref, be2_ref, o_ref, hb):
    f = pl.program_id(1)

    @pl.when(f == 0)
    def _():
        h = h_ref[...]
        hb[...] = h.astype(BF16)
        o_ref[...] = ALPHA * h + b2_ref[...]

    a = jnp.dot(hb[...], w1_ref[...], preferred_element_type=F32) + b1_ref[...]
    a = jnp.square(jnp.maximum(a, 0.0)).astype(BF16)
    o_ref[...] += jnp.dot(a, w2_ref[...], preferred_element_type=F32)

    @pl.when(f == pl.num_programs(1) - 1)
    def _():
        o_ref[...] = _layer_norm(o_ref[...], g2_ref[...], be2_ref[...])


def _ffn(h, w_ff1, b_ff1, w_ff2, b_ff2, ln2_gain, ln2_bias):
    m = h.shape[0]
    return pl.pallas_call(
        _ffn_kernel,
        grid=(m // FFN_TM, D_FF // FFN_TF),
        in_specs=[pl.BlockSpec((FFN_TM, D_MODEL), lambda i, f: (i, 0)),
                  pl.BlockSpec((D_MODEL, FFN_TF), lambda i, f: (0, f)),
                  pl.BlockSpec((1, FFN_TF), lambda i, f: (0, f)),
                  pl.BlockSpec((FFN_TF, D_MODEL), lambda i, f: (f, 0)),
                  pl.BlockSpec((1, D_MODEL), lambda i, f: (0, 0)),
                  pl.BlockSpec((1, D_MODEL), lambda i, f: (0, 0)),
                  pl.BlockSpec((1, D_MODEL), lambda i, f: (0, 0))],
        out_specs=pl.BlockSpec((FFN_TM, D_MODEL), lambda i, f: (i, 0)),
        out_shape=jax.ShapeDtypeStruct((m, D_MODEL), F32),
        scratch_shapes=[pltpu.VMEM((FFN_TM, D_MODEL), BF16)],
        compiler_params=_params(("parallel", "arbitrary")),
        name="ffn_ln2",
    )(h, w_ff1, b_ff1, w_ff2, b_ff2, ln2_gain, ln2_bias)


def kernel(x, w_in, rel_bias, ln_v_gain, ln_v_bias, w_spatial, b_spatial, w_proj_a, w_proj_b,
           w_out, ln1_gain, ln1_bias, w_ff1, b_ff1, w_ff2, b_ff2, ln2_gain, ln2_bias):
    batch, seq, d_model = x.shape
    assert (seq, d_model) == (SEQ, D_MODEL) and w_in.shape[0] == DEPTH
    bias_tiles = _bias_tiles(rel_bias)
    h = x.reshape(batch * seq, d_model)
    col_scale = jnp.concatenate([jnp.full((1, D_A), HEAD_DIM ** -0.5, F32),
                                 jnp.ones((1, D_QKV - D_A + D_REST), F32)], axis=1)
    for layer in range(DEPTH):
        w_in_b = w_in[layer].astype(BF16)
        qkv = _project(h, w_in_b, col_scale, 0, D_QKV, F32, "in_proj_qkv")
        rest = _project(h, w_in_b, col_scale, D_QKV, D_REST, BF16, "in_proj_rest")
        attn = _attention(qkv, bias_tiles, batch)
        h = _merge(attn, rest, h, ln_v_gain[layer][None, :], ln_v_bias[layer][None, :],
                   w_spatial[layer], b_spatial[layer][:, :, None],
                   w_proj_a[layer].astype(BF16), w_proj_b[layer].astype(BF16),
                   w_out[layer].astype(BF16), ln1_gain[layer][None, :], ln1_bias[layer][None, :])
        h = _ffn(h, w_ff1[layer].astype(BF16), b_ff1[layer][None, :],
                 w_ff2[layer].astype(BF16), b_ff2[layer][None, :],
                 ln2_gain[layer][None, :], ln2_bias[layer][None, :])
    return h.reshape(batch, seq, d_model)
```

```python
import math

import jax
import jax.numpy as jnp
import numpy as np
from jax import lax
from jax.experimental import pallas as pl
from jax.experimental.pallas import tpu as pltpu

D_MODEL = 2048
SEQ = 2048
HEAD_DIM = 128
N_HEADS_A = 8
D_A = N_HEADS_A * HEAD_DIM
DILATIONS = (1, 4, 16)
BLOCK = 128
N_BUCKETS = 32
MAX_DISTANCE = 2048
N_GROUPS_B = 8
D_B = N_GROUPS_B * BLOCK
D_FF = 4 * D_MODEL
D_QKV = 3 * D_A
D_REST = 2 * D_B + 2 * D_MODEL
DEPTH = 1
ALPHA = (2 * DEPTH) ** 0.25
LN_EPS = 1e-5
NEG_INF = -1e30

F32 = jnp.float32
BF16 = jnp.bfloat16

VMEM_LIMIT_BYTES = 56 * 1024 * 1024

PROJ_TM = 1024
PROJ_TN = 1024
MERGE_TM = 256
FFN_TM = 512
FFN_TF = 1024
ATTN_PIPELINE_DEPTH = 6


def _layer_norm(x, gain, bias):
    mean = jnp.mean(x, axis=-1, keepdims=True)
    xc = x - mean
    var = jnp.mean(xc * xc, axis=-1, keepdims=True)
    return xc * lax.rsqrt(var + LN_EPS) * gain + bias


def _params(semantics):
    return pltpu.CompilerParams(dimension_semantics=semantics,
                                vmem_limit_bytes=VMEM_LIMIT_BYTES)


def _bucket_tiles():
    def bucket(n):
        max_exact = N_BUCKETS // 2
        nf = np.maximum(n, 1).astype(np.float32)
        large = max_exact + (np.log(nf / np.float32(max_exact))
                             / np.float32(math.log(MAX_DISTANCE / max_exact))
                             * np.float32(N_BUCKETS - max_exact)).astype(np.int32)
        large = np.minimum(large, N_BUCKETS - 1)
        return np.where(n < max_exact, n, large)

    qi = np.arange(BLOCK)[:, None]
    kj = np.arange(2 * BLOCK)[None, :]
    steps = BLOCK + qi - kj
    band = (steps >= 0) & (steps <= BLOCK)
    tiles = []
    for dil in DILATIONS:
        t = np.where(band, bucket(np.clip(steps, 0, BLOCK) * dil), -1).astype(np.int32)
        tiles.append(t)
    return np.concatenate(tiles, axis=1)


def _bias_kernel(rb_ref, bucket_ref, out_ref):
    h = pl.program_id(0)
    bk = bucket_ref[...]
    acc = jnp.full(bk.shape, NEG_INF, F32)
    for b in range(N_BUCKETS):
        acc = jnp.where(bk == b, rb_ref[b, h], acc)
    out_ref[0] = acc


def _bias_tiles(rel_bias):
    buckets = jnp.asarray(_bucket_tiles())
    width = buckets.shape[1]
    return pl.pallas_call(
        _bias_kernel,
        grid=(N_HEADS_A,),
        in_specs=[pl.BlockSpec(memory_space=pltpu.SMEM),
                  pl.BlockSpec((BLOCK, width), lambda h: (0, 0))],
        out_specs=pl.BlockSpec((1, BLOCK, width), lambda h: (h, 0, 0)),
        out_shape=jax.ShapeDtypeStruct((N_HEADS_A, BLOCK, width), F32),
        compiler_params=_params(("arbitrary",)),
        name="bias_tiles",
    )(rel_bias, buckets)


def _proj_kernel(x_ref, w_ref, s_ref, o_ref, xb):
    @pl.when(pl.program_id(1) == 0)
    def _():
        xb[...] = x_ref[...].astype(BF16)

    acc = jnp.dot(xb[...], w_ref[...], preferred_element_type=F32)
    o_ref[...] = (acc * s_ref[...]).astype(o_ref.dtype)


def _project(x2d, w_in, col_scale, first_col, n_cols, out_dtype, name):
    m = x2d.shape[0]
    first_tile = first_col // PROJ_TN
    return pl.pallas_call(
        _proj_kernel,
        grid=(m // PROJ_TM, n_cols // PROJ_TN),
        in_specs=[pl.BlockSpec((PROJ_TM, D_MODEL), lambda i, j: (i, 0)),
                  pl.BlockSpec((D_MODEL, PROJ_TN), lambda i, j: (0, j + first_tile)),
                  pl.BlockSpec((1, PROJ_TN), lambda i, j: (0, j + first_tile))],
        out_specs=pl.BlockSpec((PROJ_TM, PROJ_TN), lambda i, j: (i, j)),
        out_shape=jax.ShapeDtypeStruct((m, n_cols), out_dtype),
        scratch_shapes=[pltpu.VMEM((PROJ_TM, D_MODEL), BF16)],
        compiler_params=_params(("parallel", "arbitrary")),
        name=name,
    )(x2d, w_in, col_scale)


def _rows(start, size, stride):
    return pl.ds(start, size) if stride == 1 else pl.ds(start, size, stride=stride)


def _attn_kernel(q_ref, k_ref, v_ref, bias_ref, o_ref, qf4, kf4, vf4, nsc, msc, dsc):
    win = 2 * BLOCK
    sub = SEQ // 4
    n_blk = sub // BLOCK

    def scores(refs, q_start, stride, first, pat, out_rows):
        qs, ks, vs = refs
        k_start = q_start if first else q_start - stride * BLOCK
        width = BLOCK if first else win
        col = pat * win + (BLOCK if first else 0)
        qb = qs[_rows(q_start, BLOCK, stride), :].astype(BF16)
        kw = ks[_rows(k_start, width, stride), :].astype(BF16)
        s = lax.dot_general(qb, kw, (((1,), (1,)), ((), ())), preferred_element_type=F32)
        s = s + bias_ref[0, :, col:col + width]
        m = jnp.max(s, axis=-1, keepdims=True)
        p = jnp.exp(s - m)
        den = jnp.sum(p, axis=-1, keepdims=True)
        msc[pat, out_rows, :] = jnp.broadcast_to(m, (BLOCK, HEAD_DIM))
        dsc[pat, out_rows, :] = jnp.broadcast_to(den, (BLOCK, HEAD_DIM))
        return p.astype(BF16), vs, _rows(k_start, width, stride), pat, out_rows

    def values(p, vs, k_rows, pat, out_rows):
        nsc[pat, out_rows, :] = jnp.dot(p, vs[k_rows, :].astype(BF16), preferred_element_type=F32)

    nat = (q_ref, k_ref, v_ref)
    mod4 = (qf4, kf4, vf4)
    blocks = []
    for n in range(SEQ // BLOCK):
        blocks.append((nat, n * BLOCK, 1, n == 0, 0, _rows(n * BLOCK, BLOCK, 1)))
    for c in range(4):
        for n in range(n_blk):
            blocks.append((nat, c + 4 * n * BLOCK, 4, n == 0, 1,
                           _rows(c * sub + n * BLOCK, BLOCK, 1)))
    for c in range(4):
        for r in range(4):
            start = c * sub + r
            blocks.append((mod4, start, 4, True, 2, _rows(start, BLOCK, 4)))

    for c in range(4):
        dst = _rows(c * sub, sub, 1)
        src = _rows(c, sub, 4)
        qf4[dst, :] = q_ref[src, :]
        kf4[dst, :] = k_ref[src, :]
        vf4[dst, :] = v_ref[src, :]

    pending = []
    for blk in blocks:
        pending.append(scores(*blk))
        if len(pending) > ATTN_PIPELINE_DEPTH:
            values(*pending.pop(0))
    for item in pending:
        values(*item)

    for c in range(4):
        for n in range(n_blk):
            tok = _rows(c + 4 * n * BLOCK, BLOCK, 4)
            m4 = _rows(c * sub + n * BLOCK, BLOCK, 1)
            m0, m1, m2 = msc[0, tok, :], msc[1, m4, :], msc[2, m4, :]
            top = jnp.maximum(jnp.maximum(m0, m1), m2)
            w0, w1, w2 = jnp.exp(m0 - top), jnp.exp(m1 - top), jnp.exp(m2 - top)
            num = w0 * nsc[0, tok, :] + w1 * nsc[1, m4, :] + w2 * nsc[2, m4, :]
            den = w0 * dsc[0, tok, :] + w1 * dsc[1, m4, :] + w2 * dsc[2, m4, :]
            o_ref[tok, :] = num / den


def _attention(qkv, bias_tiles, batch):
    width = bias_tiles.shape[-1]
    qkv_spec = lambda off: pl.BlockSpec((SEQ, HEAD_DIM), lambda b, h: (b, off + h))
    return pl.pallas_call(
        _attn_kernel,
        grid=(batch, N_HEADS_A),
        in_specs=[qkv_spec(0), qkv_spec(N_HEADS_A), qkv_spec(2 * N_HEADS_A),
                  pl.BlockSpec((1, BLOCK, width), lambda b, h: (h, 0, 0))],
        out_specs=pl.BlockSpec((SEQ, HEAD_DIM), lambda b, h: (b, h)),
        out_shape=jax.ShapeDtypeStruct((batch * SEQ, D_A), F32),
        scratch_shapes=[pltpu.VMEM((SEQ, HEAD_DIM), F32)] * 3
                       + [pltpu.VMEM((3, SEQ, HEAD_DIM), F32)] * 3,
        compiler_params=_params(("parallel", "arbitrary")),
        name="dilated_attn",
    )(qkv, qkv, qkv, bias_tiles)


def _merge_kernel(attn_ref, u_ref, vb_ref, ga_ref, gb_ref, x_ref, gv_ref, bv_ref,
                  ws_ref, bs_ref, wpa_ref, wpb_ref, wout_ref, g1_ref, b1_ref,
                  h_ref, gm):
    tm = attn_ref.shape[0]
    row = lax.broadcasted_iota(jnp.int32, (BLOCK, BLOCK), 0)
    col = lax.broadcasted_iota(jnp.int32, (BLOCK, BLOCK), 1)
    causal = col <= row
    y_a = jnp.dot(attn_ref[...].astype(BF16), wpa_ref[...], preferred_element_type=F32)
    vbn = _layer_norm(jax.nn.gelu(vb_ref[...].astype(F32)), gv_ref[...], bv_ref[...]).astype(BF16)
    for g in range(N_GROUPS_B):
        ws = jnp.where(causal, ws_ref[g], 0.0).astype(BF16)
        cols = slice(g * BLOCK, (g + 1) * BLOCK)
        for c in range(tm // BLOCK):
            rows = slice(c * BLOCK, (c + 1) * BLOCK)
            z = jnp.dot(ws, vbn[rows, cols], preferred_element_type=F32) + bs_ref[g]
            u = jax.nn.gelu(u_ref[rows, cols].astype(F32))
            gm[rows, cols] = (u * z).astype(BF16)

    y_b = jnp.dot(gm[...], wpb_ref[...], preferred_element_type=F32)
    merged = (jax.nn.sigmoid(ga_ref[...].astype(F32)) * y_a
              + jax.nn.sigmoid(gb_ref[...].astype(F32)) * y_b).astype(BF16)
    mix = jnp.dot(merged, wout_ref[...], preferred_element_type=F32)
    h_ref[...] = _layer_norm(ALPHA * x_ref[...] + mix, g1_ref[...], b1_ref[...])


def _merge(attn, rest, x2d, ln_v_gain, ln_v_bias, w_spatial, b_spatial, w_proj_a, w_proj_b, w_out,
           ln1_gain, ln1_bias):
    m = x2d.shape[0]
    tm = MERGE_TM
    const = lambda shape: pl.BlockSpec(shape, lambda i: (0,) * len(shape),
                                       pipeline_mode=pl.Buffered(1))
    return pl.pallas_call(
        _merge_kernel,
        grid=(m // tm,),
        in_specs=[pl.BlockSpec((tm, D_A), lambda i: (i, 0)),
                  pl.BlockSpec((tm, D_B), lambda i: (i, 0)),
                  pl.BlockSpec((tm, D_B), lambda i: (i, 1)),
                  pl.BlockSpec((tm, D_MODEL), lambda i: (i, 1)),
                  pl.BlockSpec((tm, D_MODEL), lambda i: (i, 2)),
                  pl.BlockSpec((tm, D_MODEL), lambda i: (i, 0)),
                  const((1, D_B)), const((1, D_B)),
                  const((N_GROUPS_B, BLOCK, BLOCK)),
                  const((N_GROUPS_B, BLOCK, 1)),
                  const((D_A, D_MODEL)), const((D_B, D_MODEL)), const((D_MODEL, D_MODEL)),
                  const((1, D_MODEL)), const((1, D_MODEL))],
        out_specs=pl.BlockSpec((tm, D_MODEL), lambda i: (i, 0)),
        out_shape=jax.ShapeDtypeStruct((m, D_MODEL), F32),
        scratch_shapes=[pltpu.VMEM((tm, D_B), BF16)],
        compiler_params=_params(("parallel",)),
        name="merge_ln1",
    )(attn, rest, rest, rest, rest, x2d, ln_v_gain, ln_v_bias,
      w_spatial, b_spatial, w_proj_a, w_proj_b, w_out, ln1_gain, ln1_bias)


def _ffn_kernel(h_ref, w1_ref, b1_ref, w2_ref, b2_ref, g2_ref, be2_ref, o_ref, hb):
    f = pl.program_id(1)

    @pl.when(f == 0)
    def _():
        h = h_ref[...]
        hb[...] = h.astype(BF16)
        o_ref[...] = ALPHA * h + b2_ref[...]

    a = jnp.dot(hb[...], w1_ref[...], preferred_element_type=F32) + b1_ref[...]
    a = jnp.square(jnp.maximum(a, 0.0)).astype(BF16)
    o_ref[...] += jnp.dot(a, w2_ref[...], preferred_element_type=F32)

    @pl.when(f == pl.num_programs(1) - 1)
    def _():
        o_ref[...] = _layer_norm(o_ref[...], g2_ref[...], be2_ref[...])


def _ffn(h, w_ff1, b_ff1, w_ff2, b_ff2, ln2_gain, ln2_bias):
    m = h.shape[0]
    return pl.pallas_call(
        _ffn_kernel,
        grid=(m // FFN_TM, D_FF // FFN_TF),
        in_specs=[pl.BlockSpec((FFN_TM, D_MODEL), lambda i, f: (i, 0)),
                  pl.BlockSpec((D_MODEL, FFN_TF), lambda i, f: (0, f)),
                  pl.BlockSpec((1, FFN_TF), lambda i, f: (0, f)),
                  pl.BlockSpec((FFN_TF, D_MODEL), lambda i, f: (f, 0)),
                  pl.BlockSpec((1, D_MODEL), lambda i, f: (0, 0)),
                  pl.BlockSpec((1, D_MODEL), lambda i, f: (0, 0)),
                  pl.BlockSpec((1, D_MODEL), lambda i, f: (0, 0))],
        out_specs=pl.BlockSpec((FFN_TM, D_MODEL), lambda i, f: (i, 0)),
        out_shape=jax.ShapeDtypeStruct((m, D_MODEL), F32),
        scratch_shapes=[pltpu.VMEM((FFN_TM, D_MODEL), BF16)],
        compiler_params=_params(("parallel", "arbitrary")),
        name="ffn_ln2",
    )(h, w_ff1, b_ff1, w_ff2, b_ff2, ln2_gain, ln2_bias)


def kernel(x, w_in, rel_bias, ln_v_gain, ln_v_bias, w_spatial, b_spatial, w_proj_a, w_proj_b,
           w_out, ln1_gain, ln1_bias, w_ff1, b_ff1, w_ff2, b_ff2, ln2_gain, ln2_bias):
    batch, seq, d_model = x.shape
    assert (seq, d_model) == (SEQ, D_MODEL) and w_in.shape[0] == DEPTH
    bias_tiles = _bias_tiles(rel_bias)
    h = x.reshape(batch * seq, d_model)
    col_scale = jnp.concatenate([jnp.full((1, D_A), HEAD_DIM ** -0.5, F32),
                                 jnp.ones((1, D_QKV - D_A + D_REST), F32)], axis=1)
    for layer in range(DEPTH):
        w_in_b = w_in[layer].astype(BF16)
        qkv = _project(h, w_in_b, col_scale, 0, D_QKV, F32, "in_proj_qkv")
        rest = _project(h, w_in_b, col_scale, D_QKV, D_REST, BF16, "in_proj_rest")
        attn = _attention(qkv, bias_tiles, batch)
        h = _merge(attn, rest, h, ln_v_gain[layer][None, :], ln_v_bias[layer][None, :],
                   w_spatial[layer], b_spatial[layer][:, :, None],
                   w_proj_a[layer].astype(BF16), w_proj_b[layer].astype(BF16),
                   w_out[layer].astype(BF16), ln1_gain[layer][None, :], ln1_bias[layer][None, :])
        h = _ffn(h, w_ff1[layer].astype(BF16), b_ff1[layer][None, :],
                 w_ff2[layer].astype(BF16), b_ff2[layer][None, :],
                 ln2_gain[layer][None, :], ln2_bias[layer][None, :])
    return h.reshape(batch, seq, d_model)
```

```python
import functools
import math

import jax
import jax.numpy as jnp
import numpy as np
from jax import lax
from jax.experimental import pallas as pl
from jax.experimental.pallas import tpu as pltpu

D_MODEL = 2048
SEQ = 2048
HEAD_DIM = 128
N_HEADS_A = 8
D_A = N_HEADS_A * HEAD_DIM
DILATIONS = (1, 4, 16)
BLOCK = 128
N_BUCKETS = 32
MAX_DISTANCE = 2048
N_GROUPS_B = 8
D_B = N_GROUPS_B * BLOCK
D_FF = 4 * D_MODEL
D_QKV = 3 * D_A
D_REST = 2 * D_B + 2 * D_MODEL
DEPTH = 1
ALPHA = (2 * DEPTH) ** 0.25
LN_EPS = 1e-5
NEG_INF = -1e30

F32 = jnp.float32
BF16 = jnp.bfloat16

VMEM_LIMIT_BYTES = 56 * 1024 * 1024

PROJ_TM = 1024
PROJ_TN = 1024
SIDE_CAST_STEPS = 64
MERGE_TM = 256
FFN_TM = 512
FFN_TF = 1024
ATTN_PIPELINE_DEPTH = 6


def _layer_norm(x, gain, bias):
    mean = jnp.mean(x, axis=-1, keepdims=True)
    xc = x - mean
    var = jnp.mean(xc * xc, axis=-1, keepdims=True)
    return xc * lax.rsqrt(var + LN_EPS) * gain + bias


def _params(semantics):
    return pltpu.CompilerParams(dimension_semantics=semantics,
                                vmem_limit_bytes=VMEM_LIMIT_BYTES)


def _bucket_tiles():
    def bucket(n):
        max_exact = N_BUCKETS // 2
        nf = np.maximum(n, 1).astype(np.float32)
        large = max_exact + (np.log(nf / np.float32(max_exact))
                             / np.float32(math.log(MAX_DISTANCE / max_exact))
                             * np.float32(N_BUCKETS - max_exact)).astype(np.int32)
        large = np.minimum(large, N_BUCKETS - 1)
        return np.where(n < max_exact, n, large)

    qi = np.arange(BLOCK)[:, None]
    kj = np.arange(2 * BLOCK)[None, :]
    steps = BLOCK + qi - kj
    band = (steps >= 0) & (steps <= BLOCK)
    tiles = []
    for dil in DILATIONS:
        t = np.where(band, bucket(np.clip(steps, 0, BLOCK) * dil), -1).astype(np.int32)
        tiles.append(t)
    return np.concatenate(tiles, axis=1)


def _bias_kernel(rb_ref, bucket_ref, out_ref):
    h = pl.program_id(0)
    bk = bucket_ref[...]
    acc = jnp.full(bk.shape, NEG_INF, F32)
    for b in range(N_BUCKETS):
        acc = jnp.where(bk == b, rb_ref[b, h], acc)
    out_ref[0] = acc


def _bias_tiles(rel_bias):
    buckets = jnp.asarray(_bucket_tiles())
    width = buckets.shape[1]
    return pl.pallas_call(
        _bias_kernel,
        grid=(N_HEADS_A,),
        in_specs=[pl.BlockSpec(memory_space=pltpu.SMEM),
                  pl.BlockSpec((BLOCK, width), lambda h: (0, 0))],
        out_specs=pl.BlockSpec((1, BLOCK, width), lambda h: (h, 0, 0)),
        out_shape=jax.ShapeDtypeStruct((N_HEADS_A, BLOCK, width), F32),
        compiler_params=_params(("arbitrary",)),
        name="bias_tiles",
    )(rel_bias, buckets)


def _proj_kernel(n_side, x_ref, w_ref, s_ref, *refs):
    side_in, o_ref, side_out, xb = refs[:n_side], refs[n_side], refs[n_side + 1:-1], refs[-1]

    @pl.when(pl.program_id(1) == 0)
    def _():
        xb[...] = x_ref[...].astype(BF16)

    acc = jnp.dot(xb[...], w_ref[...], preferred_element_type=F32)
    o_ref[...] = (acc * s_ref[...]).astype(o_ref.dtype)
    for src, dst in zip(side_in, side_out):
        dst[...] = src[...].astype(BF16)


def _project(x2d, w_in, col_scale, first_col, n_cols, out_dtype, name, side=()):
    m = x2d.shape[0]
    first_tile = first_col // PROJ_TN
    n_j = n_cols // PROJ_TN
    grid = (m // PROJ_TM, n_j)
    assert not side or grid[0] * grid[1] >= SIDE_CAST_STEPS
    side_idx = lambda i, j: (jnp.minimum(i * n_j + j, SIDE_CAST_STEPS - 1), 0)
    side_specs = [pl.BlockSpec((w.shape[0] // SIDE_CAST_STEPS, w.shape[1]), side_idx) for w in side]
    outs = pl.pallas_call(
        functools.partial(_proj_kernel, len(side)),
        grid=grid,
        in_specs=[pl.BlockSpec((PROJ_TM, D_MODEL), lambda i, j: (i, 0)),
                  pl.BlockSpec((D_MODEL, PROJ_TN), lambda i, j: (0, j + first_tile)),
                  pl.BlockSpec((1, PROJ_TN), lambda i, j: (0, j + first_tile))] + side_specs,
        out_specs=[pl.BlockSpec((PROJ_TM, PROJ_TN), lambda i, j: (i, j))] + side_specs,
        out_shape=[jax.ShapeDtypeStruct((m, n_cols), out_dtype)]
                  + [jax.ShapeDtypeStruct(w.shape, BF16) for w in side],
        scratch_shapes=[pltpu.VMEM((PROJ_TM, D_MODEL), BF16)],
        compiler_params=_params(("arbitrary", "arbitrary")),
        name=name,
    )(x2d, w_in, col_scale, *side)
    return outs[0] if not side else outs


def _rows(start, size, stride):
    return pl.ds(start, size) if stride == 1 else pl.ds(start, size, stride=stride)


def _attn_kernel(q_ref, k_ref, v_ref, bias_ref, o_ref, qf4, kf4, vf4, nsc, msc, dsc):
    win = 2 * BLOCK
    sub = SEQ // 4
    n_blk = sub // BLOCK

    def scores(refs, q_start, stride, first, pat, out_rows):
        qs, ks, vs = refs
        k_start = q_start if first else q_start - stride * BLOCK
        width = BLOCK if first else win
        col = pat * win + (BLOCK if first else 0)
        qb = qs[_rows(q_start, BLOCK, stride), :].astype(BF16)
        kw = ks[_rows(k_start, width, stride), :].astype(BF16)
        s = lax.dot_general(qb, kw, (((1,), (1,)), ((), ())), preferred_element_type=F32)
        s = s + bias_ref[0, :, col:col + width]
        m = jnp.max(s, axis=-1, keepdims=True)
        p = jnp.exp(s - m)
        den = jnp.sum(p, axis=-1, keepdims=True)
        msc[pat, out_rows, :] = jnp.broadcast_to(m, (BLOCK, HEAD_DIM))
        dsc[pat, out_rows, :] = jnp.broadcast_to(den, (BLOCK, HEAD_DIM))
        return p.astype(BF16), vs, _rows(k_start, width, stride), pat, out_rows

    def values(p, vs, k_rows, pat, out_rows):
        nsc[pat, out_rows, :] = jnp.dot(p, vs[k_rows, :].astype(BF16), preferred_element_type=F32)

    nat = (q_ref, k_ref, v_ref)
    mod4 = (qf4, kf4, vf4)
    blocks = []
    for n in range(SEQ // BLOCK):
        blocks.append((nat, n * BLOCK, 1, n == 0, 0, _rows(n * BLOCK, BLOCK, 1)))
    for c in range(4):
        for n in range(n_blk):
            blocks.append((nat, c + 4 * n * BLOCK, 4, n == 0, 1,
                           _rows(c * sub + n * BLOCK, BLOCK, 1)))
    for c in range(4):
        for r in range(4):
            start = c * sub + r
            blocks.append((mod4, start, 4, True, 2, _rows(start, BLOCK, 4)))

    for c in range(4):
        dst = _rows(c * sub, sub, 1)
        src = _rows(c, sub, 4)
        qf4[dst, :] = q_ref[src, :]
        kf4[dst, :] = k_ref[src, :]
        vf4[dst, :] = v_ref[src, :]

    pending = []
    for blk in blocks:
        pending.append(scores(*blk))
        if len(pending) > ATTN_PIPELINE_DEPTH:
            values(*pending.pop(0))
    for item in pending:
        values(*item)

    for c in range(4):
        for n in range(n_blk):
            tok = _rows(c + 4 * n * BLOCK, BLOCK, 4)
            m4 = _rows(c * sub + n * BLOCK, BLOCK, 1)
            m0, m1, m2 = msc[0, tok, :], msc[1, m4, :], msc[2, m4, :]
            top = jnp.maximum(jnp.maximum(m0, m1), m2)
            w0, w1, w2 = jnp.exp(m0 - top), jnp.exp(m1 - top), jnp.exp(m2 - top)
            num = w0 * nsc[0, tok, :] + w1 * nsc[1, m4, :] + w2 * nsc[2, m4, :]
            den = w0 * dsc[0, tok, :] + w1 * dsc[1, m4, :] + w2 * dsc[2, m4, :]
            o_ref[tok, :] = num / den


def _attention(qkv, bias_tiles, batch):
    width = bias_tiles.shape[-1]
    qkv_spec = lambda off: pl.BlockSpec((SEQ, HEAD_DIM), lambda b, h: (b, off + h))
    return pl.pallas_call(
        _attn_kernel,
        grid=(batch, N_HEADS_A),
        in_specs=[qkv_spec(0), qkv_spec(N_HEADS_A), qkv_spec(2 * N_HEADS_A),
                  pl.BlockSpec((1, BLOCK, width), lambda b, h: (h, 0, 0))],
        out_specs=pl.BlockSpec((SEQ, HEAD_DIM), lambda b, h: (b, h)),
        out_shape=jax.ShapeDtypeStruct((batch * SEQ, D_A), F32),
        scratch_shapes=[pltpu.VMEM((SEQ, HEAD_DIM), F32)] * 3
                       + [pltpu.VMEM((3, SEQ, HEAD_DIM), F32)] * 3,
        compiler_params=_params(("parallel", "arbitrary")),
        name="dilated_attn",
    )(qkv, qkv, qkv, bias_tiles)


def _merge_kernel(attn_ref, u_ref, vb_ref, ga_ref, gb_ref, x_ref, gv_ref, bv_ref,
                  ws_ref, bs_ref, wpa_ref, wpb_ref, wout_ref, g1_ref, b1_ref,
                  h_ref, gm):
    tm = attn_ref.shape[0]
    row = lax.broadcasted_iota(jnp.int32, (BLOCK, BLOCK), 0)
    col = lax.broadcasted_iota(jnp.int32, (BLOCK, BLOCK), 1)
    causal = col <= row
    y_a = jnp.dot(attn_ref[...].astype(BF16), wpa_ref[...], preferred_element_type=F32)
    vbn = _layer_norm(jax.nn.gelu(vb_ref[...].astype(F32)), gv_ref[...], bv_ref[...]).astype(BF16)
    for g in range(N_GROUPS_B):
        ws = jnp.where(causal, ws_ref[g], 0.0).astype(BF16)
        cols = slice(g * BLOCK, (g + 1) * BLOCK)
        for c in range(0, tm // BLOCK, 2):
            rows = [slice((c + k) * BLOCK, (c + k + 1) * BLOCK) for k in range(2)]
            pair = jnp.concatenate([vbn[rows[0], cols], vbn[rows[1], cols]], axis=1)
            z = jnp.dot(ws, pair, preferred_element_type=F32) + bs_ref[g]
            for k in range(2):
                u = jax.nn.gelu(u_ref[rows[k], cols].astype(F32))
                gm[rows[k], cols] = (u * z[:, k * BLOCK:(k + 1) * BLOCK]).astype(BF16)

    y_b = jnp.dot(gm[...], wpb_ref[...], preferred_element_type=F32)
    merged = (jax.nn.sigmoid(ga_ref[...].astype(F32)) * y_a
              + jax.nn.sigmoid(gb_ref[...].astype(F32)) * y_b).astype(BF16)
    mix = jnp.dot(merged, wout_ref[...], preferred_element_type=F32)
    h_ref[...] = _layer_norm(ALPHA * x_ref[...] + mix, g1_ref[...], b1_ref[...])


def _merge(attn, rest, x2d, ln_v_gain, ln_v_bias, w_spatial, b_spatial, w_proj_a, w_proj_b, w_out,
           ln1_gain, ln1_bias):
    m = x2d.shape[0]
    tm = MERGE_TM
    const = lambda shape: pl.BlockSpec(shape, lambda i: (0,) * len(shape),
                                       pipeline_mode=pl.Buffered(1))
    return pl.pallas_call(
        _merge_kernel,
        grid=(m // tm,),
        in_specs=[pl.BlockSpec((tm, D_A), lambda i: (i, 0)),
                  pl.BlockSpec((tm, D_B), lambda i: (i, 0)),
                  pl.BlockSpec((tm, D_B), lambda i: (i, 1)),
                  pl.BlockSpec((tm, D_MODEL), lambda i: (i, 1)),
                  pl.BlockSpec((tm, D_MODEL), lambda i: (i, 2)),
                  pl.BlockSpec((tm, D_MODEL), lambda i: (i, 0)),
                  const((1, D_B)), const((1, D_B)),
                  const((N_GROUPS_B, BLOCK, BLOCK)),
                  const((N_GROUPS_B, BLOCK, 1)),
                  const((D_A, D_MODEL)), const((D_B, D_MODEL)), const((D_MODEL, D_MODEL)),
                  const((1, D_MODEL)), const((1, D_MODEL))],
        out_specs=pl.BlockSpec((tm, D_MODEL), lambda i: (i, 0)),
        out_shape=jax.ShapeDtypeStruct((m, D_MODEL), F32),
        scratch_shapes=[pltpu.VMEM((tm, D_B), BF16)],
        compiler_params=_params(("parallel",)),
        name="merge_ln1",
    )(attn, rest, rest, rest, rest, x2d, ln_v_gain, ln_v_bias,
      w_spatial, b_spatial, w_proj_a, w_proj_b, w_out, ln1_gain, ln1_bias)


def _ffn_kernel(h_ref, w1_ref, b1_ref, w2_ref, b2_ref, g2_ref, be2_ref, o_ref, hb):
    f = pl.program_id(1)

    @pl.when(f == 0)
    def _():
        h = h_ref[...]
        hb[...] = h.astype(BF16)
        o_ref[...] = ALPHA * h + b2_ref[...]

    a = jnp.dot(hb[...], w1_ref[...], preferred_element_type=F32) + b1_ref[...]
    a = jnp.square(jnp.maximum(a, 0.0)).astype(BF16)
    o_ref[...] += jnp.dot(a, w2_ref[...], preferred_element_type=F32)

    @pl.when(f == pl.num_programs(1) - 1)
    def _():
        o_ref[...] = _layer_norm(o_ref[...], g2_ref[...], be2_ref[...])


def _ffn(h, w_ff1, b_ff1, w_ff2, b_ff2, ln2_gain, ln2_bias):
    m = h.shape[0]
    return pl.pallas_call(
        _ffn_kernel,
        grid=(m // FFN_TM, D_FF // FFN_TF),
        in_specs=[pl.BlockSpec((FFN_TM, D_MODEL), lambda i, f: (i, 0)),
                  pl.BlockSpec((D_MODEL, FFN_TF), lambda i, f: (0, f)),
                  pl.BlockSpec((1, FFN_TF), lambda i, f: (0, f)),
                  pl.BlockSpec((FFN_TF, D_MODEL), lambda i, f: (f, 0)),
                  pl.BlockSpec((1, D_MODEL), lambda i, f: (0, 0)),
                  pl.BlockSpec((1, D_MODEL), lambda i, f: (0, 0)),
                  pl.BlockSpec((1, D_MODEL), lambda i, f: (0, 0))],
        out_specs=pl.BlockSpec((FFN_TM, D_MODEL), lambda i, f: (i, 0)),
        out_shape=jax.ShapeDtypeStruct((m, D_MODEL), F32),
        scratch_shapes=[pltpu.VMEM((FFN_TM, D_MODEL), BF16)],
        compiler_params=_params(("parallel", "arbitrary")),
        name="ffn_ln2",
    )(h, w_ff1, b_ff1, w_ff2, b_ff2, ln2_gain, ln2_bias)


def kernel(x, w_in, rel_bias, ln_v_gain, ln_v_bias, w_spatial, b_spatial, w_proj_a, w_proj_b,
           w_out, ln1_gain, ln1_bias, w_ff1, b_ff1, w_ff2, b_ff2, ln2_gain, ln2_bias):
    batch, seq, d_model = x.shape
    assert (seq, d_model) == (SEQ, D_MODEL) and w_in.shape[0] == DEPTH
    bias_tiles = _bias_tiles(rel_bias)
    h = x.reshape(batch * seq, d_model)
    col_scale = jnp.concatenate([jnp.full((1, D_A), HEAD_DIM ** -0.5, F32),
                                 jnp.ones((1, D_QKV - D_A + D_REST), F32)], axis=1)
    for layer in range(DEPTH):
        w_in_b = w_in[layer].astype(BF16)
        qkv = _project(h, w_in_b, col_scale, 0, D_QKV, F32, "in_proj_qkv")
        rest, w_pa_b, w_pb_b, w_out_b, w_ff1_b, w_ff2_b = _project(
            h, w_in_b, col_scale, D_QKV, D_REST, BF16, "in_proj_rest",
            side=(w_proj_a[layer], w_proj_b[layer], w_out[layer], w_ff1[layer], w_ff2[layer]))
        attn = _attention(qkv, bias_tiles, batch)
        h = _merge(attn, rest, h, ln_v_gain[layer][None, :], ln_v_bias[layer][None, :],
                   w_spatial[layer], b_spatial[layer][:, :, None], w_pa_b, w_pb_b, w_out_b,
                   ln1_gain[layer][None, :], ln1_bias[layer][None, :])
        h = _ffn(h, w_ff1_b, b_ff1[layer][None, :], w_ff2_b, b_ff2[layer][None, :],
                 ln2_gain[layer][None, :], ln2_bias[layer][None, :])
    return h.reshape(batch, seq, d_model)
```

```python
import functools
import math

import jax
import jax.numpy as jnp
import numpy as np
from jax import lax
from jax.experimental import pallas as pl
from jax.experimental.pallas import tpu as pltpu

D_MODEL = 2048
SEQ = 2048
HEAD_DIM = 128
N_HEADS_A = 8
D_A = N_HEADS_A * HEAD_DIM
DILATIONS = (1, 4, 16)
BLOCK = 128
N_BUCKETS = 32
MAX_DISTANCE = 2048
N_GROUPS_B = 8
D_B = N_GROUPS_B * BLOCK
D_FF = 4 * D_MODEL
D_QKV = 3 * D_A
D_REST = 2 * D_B + 2 * D_MODEL
DEPTH = 1
ALPHA = (2 * DEPTH) ** 0.25
LN_EPS = 1e-5
NEG_INF = -1e30

F32 = jnp.float32
BF16 = jnp.bfloat16

VMEM_LIMIT_BYTES = 56 * 1024 * 1024

PROJ_TM = 1024
PROJ_TN = 1024
SIDE_CAST_STEPS = 64
MERGE_TM = 256
FFN_TM = 512
FFN_TF = 1024
ATTN_PIPELINE_DEPTH = 6


def _layer_norm(x, gain, bias):
    mean = jnp.mean(x, axis=-1, keepdims=True)
    xc = x - mean
    var = jnp.mean(xc * xc, axis=-1, keepdims=True)
    return xc * lax.rsqrt(var + LN_EPS) * gain + bias


def _params(semantics):
    return pltpu.CompilerParams(dimension_semantics=semantics,
                                vmem_limit_bytes=VMEM_LIMIT_BYTES)


def _bucket_tiles():
    def bucket(n):
        max_exact = N_BUCKETS // 2
        nf = np.maximum(n, 1).astype(np.float32)
        large = max_exact + (np.log(nf / np.float32(max_exact))
                             / np.float32(math.log(MAX_DISTANCE / max_exact))
                             * np.float32(N_BUCKETS - max_exact)).astype(np.int32)
        large = np.minimum(large, N_BUCKETS - 1)
        return np.where(n < max_exact, n, large)

    qi = np.arange(BLOCK)[:, None]
    kj = np.arange(2 * BLOCK)[None, :]
    steps = BLOCK + qi - kj
    band = (steps >= 0) & (steps <= BLOCK)
    tiles = []
    for dil in DILATIONS:
        t = np.where(band, bucket(np.clip(steps, 0, BLOCK) * dil), -1).astype(np.int32)
        tiles.append(t)
    return np.concatenate(tiles, axis=1)


def _bias_kernel(rb_ref, bucket_ref, out_ref):
    h = pl.program_id(0)
    bk = bucket_ref[...]
    acc = jnp.full(bk.shape, NEG_INF, F32)
    for b in range(N_BUCKETS):
        acc = jnp.where(bk == b, rb_ref[b, h], acc)
    out_ref[0] = acc


def _bias_tiles(rel_bias):
    buckets = jnp.asarray(_bucket_tiles())
    width = buckets.shape[1]
    return pl.pallas_call(
        _bias_kernel,
        grid=(N_HEADS_A,),
        in_specs=[pl.BlockSpec(memory_space=pltpu.SMEM),
                  pl.BlockSpec((BLOCK, width), lambda h: (0, 0))],
        out_specs=pl.BlockSpec((1, BLOCK, width), lambda h: (h, 0, 0)),
        out_shape=jax.ShapeDtypeStruct((N_HEADS_A, BLOCK, width), F32),
        compiler_params=_params(("arbitrary",)),
        name="bias_tiles",
    )(rel_bias, buckets)


def _proj_kernel(n_side, x_ref, w_ref, s_ref, *refs):
    side_in, o_ref, side_out, xb = refs[:n_side], refs[n_side], refs[n_side + 1:-1], refs[-1]

    @pl.when(pl.program_id(1) == 0)
    def _():
        xb[...] = x_ref[...].astype(BF16)

    acc = jnp.dot(xb[...], w_ref[...], preferred_element_type=F32)
    o_ref[...] = (acc * s_ref[...]).astype(o_ref.dtype)
    for src, dst in zip(side_in, side_out):
        dst[...] = src[...].astype(BF16)


def _project(x2d, w_in, col_scale, first_col, n_cols, out_dtype, name, side=()):
    m = x2d.shape[0]
    first_tile = first_col // PROJ_TN
    n_j = n_cols // PROJ_TN
    grid = (m // PROJ_TM, n_j)
    assert not side or grid[0] * grid[1] >= SIDE_CAST_STEPS
    side_idx = lambda i, j: (jnp.minimum(i * n_j + j, SIDE_CAST_STEPS - 1), 0)
    side_specs = [pl.BlockSpec((w.shape[0] // SIDE_CAST_STEPS, w.shape[1]), side_idx) for w in side]
    outs = pl.pallas_call(
        functools.partial(_proj_kernel, len(side)),
        grid=grid,
        in_specs=[pl.BlockSpec((PROJ_TM, D_MODEL), lambda i, j: (i, 0)),
                  pl.BlockSpec((D_MODEL, PROJ_TN), lambda i, j: (0, j + first_tile)),
                  pl.BlockSpec((1, PROJ_TN), lambda i, j: (0, j + first_tile))] + side_specs,
        out_specs=[pl.BlockSpec((PROJ_TM, PROJ_TN), lambda i, j: (i, j))] + side_specs,
        out_shape=[jax.ShapeDtypeStruct((m, n_cols), out_dtype)]
                  + [jax.ShapeDtypeStruct(w.shape, BF16) for w in side],
        scratch_shapes=[pltpu.VMEM((PROJ_TM, D_MODEL), BF16)],
        compiler_params=_params(("arbitrary", "arbitrary")),
        name=name,
    )(x2d, w_in, col_scale, *side)
    return outs[0] if not side else outs


def _rows(start, size, stride):
    return pl.ds(start, size) if stride == 1 else pl.ds(start, size, stride=stride)


def _attn_kernel(q_ref, k_ref, v_ref, bias_ref, o_ref, qf4, kf4, vf4, nsc, msc, dsc):
    win = 2 * BLOCK
    sub = SEQ // 4
    n_blk = sub // BLOCK

    def scores(refs, q_start, stride, first, pat, out_rows):
        qs, ks, vs = refs
        k_start = q_start if first else q_start - stride * BLOCK
        width = BLOCK if first else win
        col = pat * win + (BLOCK if first else 0)
        qb = qs[_rows(q_start, BLOCK, stride), :].astype(BF16)
        kw = ks[_rows(k_start, width, stride), :].astype(BF16)
        s = lax.dot_general(qb, kw, (((1,), (1,)), ((), ())), preferred_element_type=F32)
        s = s + bias_ref[0, :, col:col + width]
        m = jnp.max(s, axis=-1, keepdims=True)
        p = jnp.exp(s - m).astype(BF16)
        msc[pat, out_rows, :] = jnp.broadcast_to(m, (BLOCK, HEAD_DIM))
        return p, vs, _rows(k_start, width, stride), pat, out_rows

    def values(p, vs, k_rows, pat, out_rows):
        vw = vs[k_rows, :].astype(BF16)
        nd = jnp.dot(p, jnp.concatenate([vw, jnp.ones_like(vw)], axis=1),
                     preferred_element_type=F32)
        nsc[pat, out_rows, :] = nd[:, :HEAD_DIM]
        dsc[pat, out_rows, :] = nd[:, HEAD_DIM:]

    nat = (q_ref, k_ref, v_ref)
    mod4 = (qf4, kf4, vf4)
    blocks = []
    for n in range(SEQ // BLOCK):
        blocks.append((nat, n * BLOCK, 1, n == 0, 0, _rows(n * BLOCK, BLOCK, 1)))
    for c in range(4):
        for n in range(n_blk):
            blocks.append((nat, c + 4 * n * BLOCK, 4, n == 0, 1,
                           _rows(c * sub + n * BLOCK, BLOCK, 1)))
    for c in range(4):
        for r in range(4):
            start = c * sub + r
            blocks.append((mod4, start, 4, True, 2, _rows(start, BLOCK, 4)))

    for c in range(4):
        dst = _rows(c * sub, sub, 1)
        src = _rows(c, sub, 4)
        qf4[dst, :] = q_ref[src, :]
        kf4[dst, :] = k_ref[src, :]
        vf4[dst, :] = v_ref[src, :]

    pending = []
    for blk in blocks:
        pending.append(scores(*blk))
        if len(pending) > ATTN_PIPELINE_DEPTH:
            values(*pending.pop(0))
    for item in pending:
        values(*item)

    for c in range(4):
        for n in range(n_blk):
            tok = _rows(c + 4 * n * BLOCK, BLOCK, 4)
            m4 = _rows(c * sub + n * BLOCK, BLOCK, 1)
            m0, m1, m2 = msc[0, tok, :], msc[1, m4, :], msc[2, m4, :]
            top = jnp.maximum(jnp.maximum(m0, m1), m2)
            w0, w1, w2 = jnp.exp(m0 - top), jnp.exp(m1 - top), jnp.exp(m2 - top)
            num = w0 * nsc[0, tok, :] + w1 * nsc[1, m4, :] + w2 * nsc[2, m4, :]
            den = w0 * dsc[0, tok, :] + w1 * dsc[1, m4, :] + w2 * dsc[2, m4, :]
            o_ref[tok, :] = num / den


def _attention(qkv, bias_tiles, batch):
    width = bias_tiles.shape[-1]
    qkv_spec = lambda off: pl.BlockSpec((SEQ, HEAD_DIM), lambda b, h: (b, off + h))
    return pl.pallas_call(
        _attn_kernel,
        grid=(batch, N_HEADS_A),
        in_specs=[qkv_spec(0), qkv_spec(N_HEADS_A), qkv_spec(2 * N_HEADS_A),
                  pl.BlockSpec((1, BLOCK, width), lambda b, h: (h, 0, 0))],
        out_specs=pl.BlockSpec((SEQ, HEAD_DIM), lambda b, h: (b, h)),
        out_shape=jax.ShapeDtypeStruct((batch * SEQ, D_A), F32),
        scratch_shapes=[pltpu.VMEM((SEQ, HEAD_DIM), F32)] * 3
                       + [pltpu.VMEM((3, SEQ, HEAD_DIM), F32)] * 3,
        compiler_params=_params(("parallel", "arbitrary")),
        name="dilated_attn",
    )(qkv, qkv, qkv, bias_tiles)


def _merge_kernel(attn_ref, u_ref, vb_ref, ga_ref, gb_ref, x_ref, gv_ref, bv_ref,
                  ws_ref, bs_ref, wpa_ref, wpb_ref, wout_ref, g1_ref, b1_ref,
                  h_ref, gm):
    tm = attn_ref.shape[0]
    row = lax.broadcasted_iota(jnp.int32, (BLOCK, BLOCK), 0)
    col = lax.broadcasted_iota(jnp.int32, (BLOCK, BLOCK), 1)
    causal = col <= row
    y_a = jnp.dot(attn_ref[...].astype(BF16), wpa_ref[...], preferred_element_type=F32)
    vbn = _layer_norm(jax.nn.gelu(vb_ref[...].astype(F32)), gv_ref[...], bv_ref[...]).astype(BF16)
    for g in range(N_GROUPS_B):
        ws = jnp.where(causal, ws_ref[g], 0.0).astype(BF16)
        cols = slice(g * BLOCK, (g + 1) * BLOCK)
        for c in range(0, tm // BLOCK, 2):
            rows = [slice((c + k) * BLOCK, (c + k + 1) * BLOCK) for k in range(2)]
            pair = jnp.concatenate([vbn[rows[0], cols], vbn[rows[1], cols]], axis=1)
            z = jnp.dot(ws, pair, preferred_element_type=F32) + bs_ref[g]
            for k in range(2):
                u = jax.nn.gelu(u_ref[rows[k], cols].astype(F32))
                gm[rows[k], cols] = (u * z[:, k * BLOCK:(k + 1) * BLOCK]).astype(BF16)

    y_b = jnp.dot(gm[...], wpb_ref[...], preferred_element_type=F32)
    merged = (jax.nn.sigmoid(ga_ref[...].astype(F32)) * y_a
              + jax.nn.sigmoid(gb_ref[...].astype(F32)) * y_b).astype(BF16)
    mix = jnp.dot(merged, wout_ref[...], preferred_element_type=F32)
    h_ref[...] = _layer_norm(ALPHA * x_ref[...] + mix, g1_ref[...], b1_ref[...])


def _merge(attn, rest, x2d, ln_v_gain, ln_v_bias, w_spatial, b_spatial, w_proj_a, w_proj_b, w_out,
           ln1_gain, ln1_bias):
    m = x2d.shape[0]
    tm = MERGE_TM
    const = lambda shape: pl.BlockSpec(shape, lambda i: (0,) * len(shape),
                                       pipeline_mode=pl.Buffered(1))
    return pl.pallas_call(
        _merge_kernel,
        grid=(m // tm,),
        in_specs=[pl.BlockSpec((tm, D_A), lambda i: (i, 0)),
                  pl.BlockSpec((tm, D_B), lambda i: (i, 0)),
                  pl.BlockSpec((tm, D_B), lambda i: (i, 1)),
                  pl.BlockSpec((tm, D_MODEL), lambda i: (i, 1)),
                  pl.BlockSpec((tm, D_MODEL), lambda i: (i, 2)),
                  pl.BlockSpec((tm, D_MODEL), lambda i: (i, 0)),
                  const((1, D_B)), const((1, D_B)),
                  const((N_GROUPS_B, BLOCK, BLOCK)),
                  const((N_GROUPS_B, BLOCK, 1)),
                  const((D_A, D_MODEL)), const((D_B, D_MODEL)), const((D_MODEL, D_MODEL)),
                  const((1, D_MODEL)), const((1, D_MODEL))],
        out_specs=pl.BlockSpec((tm, D_MODEL), lambda i: (i, 0)),
        out_shape=jax.ShapeDtypeStruct((m, D_MODEL), F32),
        scratch_shapes=[pltpu.VMEM((tm, D_B), BF16)],
        compiler_params=_params(("parallel",)),
        name="merge_ln1",
    )(attn, rest, rest, rest, rest, x2d, ln_v_gain, ln_v_bias,
      w_spatial, b_spatial, w_proj_a, w_proj_b, w_out, ln1_gain, ln1_bias)


def _ffn_kernel(h_ref, w1_ref, b1_ref, w2_ref, b2_ref, g2_ref, be2_ref, o_ref, hb):
    f = pl.program_id(1)

    @pl.when(f == 0)
    def _():
        h = h_ref[...]
        hb[...] = h.astype(BF16)
        o_ref[...] = ALPHA * h + b2_ref[...]

    a = jnp.dot(hb[...], w1_ref[...], preferred_element_type=F32) + b1_ref[...]
    a = jnp.square(jnp.maximum(a, 0.0)).astype(BF16)
    o_ref[...] += jnp.dot(a, w2_ref[...], preferred_element_type=F32)

    @pl.when(f == pl.num_programs(1) - 1)
    def _():
        o_ref[...] = _layer_norm(o_ref[...], g2_ref[...], be2_ref[...])


def _ffn(h, w_ff1, b_ff1, w_ff2, b_ff2, ln2_gain, ln2_bias):
    m = h.shape[0]
    return pl.pallas_call(
        _ffn_kernel,
        grid=(m // FFN_TM, D_FF // FFN_TF),
        in_specs=[pl.BlockSpec((FFN_TM, D_MODEL), lambda i, f: (i, 0)),
                  pl.BlockSpec((D_MODEL, FFN_TF), lambda i, f: (0, f)),
                  pl.BlockSpec((1, FFN_TF), lambda i, f: (0, f)),
                  pl.BlockSpec((FFN_TF, D_MODEL), lambda i, f: (f, 0)),
                  pl.BlockSpec((1, D_MODEL), lambda i, f: (0, 0)),
                  pl.BlockSpec((1, D_MODEL), lambda i, f: (0, 0)),
                  pl.BlockSpec((1, D_MODEL), lambda i, f: (0, 0))],
        out_specs=pl.BlockSpec((FFN_TM, D_MODEL), lambda i, f: (i, 0)),
        out_shape=jax.ShapeDtypeStruct((m, D_MODEL), F32),
        scratch_shapes=[pltpu.VMEM((FFN_TM, D_MODEL), BF16)],
        compiler_params=_params(("parallel", "arbitrary")),
        name="ffn_ln2",
    )(h, w_ff1, b_ff1, w_ff2, b_ff2, ln2_gain, ln2_bias)


def kernel(x, w_in, rel_bias, ln_v_gain, ln_v_bias, w_spatial, b_spatial, w_proj_a, w_proj_b,
           w_out, ln1_gain, ln1_bias, w_ff1, b_ff1, w_ff2, b_ff2, ln2_gain, ln2_bias):
    batch, seq, d_model = x.shape
    assert (seq, d_model) == (SEQ, D_MODEL) and w_in.shape[0] == DEPTH
    bias_tiles = _bias_tiles(rel_bias)
    h = x.reshape(batch * seq, d_model)
    col_scale = jnp.concatenate([jnp.full((1, D_A), HEAD_DIM ** -0.5, F32),
                                 jnp.ones((1, D_QKV - D_A + D_REST), F32)], axis=1)
    for layer in range(DEPTH):
        w_in_b = w_in[layer].astype(BF16)
        qkv = _project(h, w_in_b, col_scale, 0, D_QKV, F32, "in_proj_qkv")
        rest, w_pa_b, w_pb_b, w_out_b, w_ff1_b, w_ff2_b = _project(
            h, w_in_b, col_scale, D_QKV, D_REST, BF16, "in_proj_rest",
            side=(w_proj_a[layer], w_proj_b[layer], w_out[layer], w_ff1[layer], w_ff2[layer]))
        attn = _attention(qkv, bias_tiles, batch)
        h = _merge(attn, rest, h, ln_v_gain[layer][None, :], ln_v_bias[layer][None, :],
                   w_spatial[layer], b_spatial[layer][:, :, None], w_pa_b, w_pb_b, w_out_b,
                   ln1_gain[layer][None, :], ln1_bias[layer][None, :])
        h = _ffn(h, w_ff1_b, b_ff1[layer][None, :], w_ff2_b, b_ff2[layer][None, :],
                 ln2_gain[layer][None, :], ln2_bias[layer][None, :])
    return h.reshape(batch, seq, d_model)
```

```python
import functools
import math

import jax
import jax.numpy as jnp
import numpy as np
from jax import lax
from jax.experimental import pallas as pl
from jax.experimental.pallas import tpu as pltpu

D_MODEL = 2048
SEQ = 2048
HEAD_DIM = 128
N_HEADS_A = 8
D_A = N_HEADS_A * HEAD_DIM
DILATIONS = (1, 4, 16)
BLOCK = 128
N_BUCKETS = 32
MAX_DISTANCE = 2048
N_GROUPS_B = 8
D_B = N_GROUPS_B * BLOCK
D_FF = 4 * D_MODEL
D_QKV = 3 * D_A
D_REST = 2 * D_B + 2 * D_MODEL
DEPTH = 1
ALPHA = (2 * DEPTH) ** 0.25
LN_EPS = 1e-5
NEG_INF = -1e30

F32 = jnp.float32
BF16 = jnp.bfloat16

VMEM_LIMIT_BYTES = 56 * 1024 * 1024

PROJ_TM = 1024
PROJ_TN = 1024
SIDE_CAST_STEPS = 64
MERGE_TM = 256
FFN_TM = 512
FFN_TF = 1024
ATTN_PIPELINE_DEPTH = 6


def _layer_norm(x, gain, bias):
    mean = jnp.mean(x, axis=-1, keepdims=True)
    xc = x - mean
    var = jnp.mean(xc * xc, axis=-1, keepdims=True)
    return xc * lax.rsqrt(var + LN_EPS) * gain + bias


def _sigmoid(x):
    return 0.5 + 0.5 * jnp.tanh(0.5 * x)


def _gelu(x):
    c = math.sqrt(2.0 / math.pi)
    t = jnp.tanh(x * (c + (c * 0.044715) * (x * x)))
    half = 0.5 * x
    return half + half * t


def _params(semantics):
    return pltpu.CompilerParams(dimension_semantics=semantics,
                                vmem_limit_bytes=VMEM_LIMIT_BYTES)


def _bucket_tiles():
    def bucket(n):
        max_exact = N_BUCKETS // 2
        nf = np.maximum(n, 1).astype(np.float32)
        large = max_exact + (np.log(nf / np.float32(max_exact))
                             / np.float32(math.log(MAX_DISTANCE / max_exact))
                             * np.float32(N_BUCKETS - max_exact)).astype(np.int32)
        large = np.minimum(large, N_BUCKETS - 1)
        return np.where(n < max_exact, n, large)

    qi = np.arange(BLOCK)[:, None]
    kj = np.arange(2 * BLOCK)[None, :]
    steps = BLOCK + qi - kj
    band = (steps >= 0) & (steps <= BLOCK)
    tiles = []
    for dil in DILATIONS:
        t = np.where(band, bucket(np.clip(steps, 0, BLOCK) * dil), -1).astype(np.int32)
        tiles.append(t)
    return np.concatenate(tiles, axis=1)


def _bias_kernel(rb_ref, bucket_ref, out_ref):
    h = pl.program_id(0)
    bk = bucket_ref[...]
    acc = jnp.full(bk.shape, NEG_INF, F32)
    for b in range(N_BUCKETS):
        acc = jnp.where(bk == b, rb_ref[b, h], acc)
    out_ref[0] = acc


def _bias_tiles(rel_bias):
    buckets = jnp.asarray(_bucket_tiles())
    width = buckets.shape[1]
    return pl.pallas_call(
        _bias_kernel,
        grid=(N_HEADS_A,),
        in_specs=[pl.BlockSpec(memory_space=pltpu.SMEM),
                  pl.BlockSpec((BLOCK, width), lambda h: (0, 0))],
        out_specs=pl.BlockSpec((1, BLOCK, width), lambda h: (h, 0, 0)),
        out_shape=jax.ShapeDtypeStruct((N_HEADS_A, BLOCK, width), F32),
        compiler_params=_params(("arbitrary",)),
        name="bias_tiles",
    )(rel_bias, buckets)


def _proj_kernel(n_side, x_ref, w_ref, s_ref, *refs):
    side_in, (qkv_ref, rest_ref) = refs[:n_side], refs[n_side:n_side + 2]
    side_out, xb = refs[n_side + 2:-1], refs[-1]

    @pl.when(pl.program_id(1) == 0)
    def _():
        xb[...] = x_ref[...].astype(BF16)

    acc = jnp.dot(xb[...], w_ref[...], preferred_element_type=F32) * s_ref[...]
    qkv_ref[...] = acc
    rest_ref[...] = acc.astype(BF16)
    for src, dst in zip(side_in, side_out):
        dst[...] = src[...].astype(BF16)


def _project(x2d, w_in, col_scale, side):
    m = x2d.shape[0]
    n_qkv = D_QKV // PROJ_TN
    n_j = (D_QKV + D_REST) // PROJ_TN
    grid = (m // PROJ_TM, n_j)
    assert grid[0] * grid[1] >= SIDE_CAST_STEPS
    side_idx = lambda i, j: (jnp.minimum(i * n_j + j, SIDE_CAST_STEPS - 1), 0)
    side_specs = [pl.BlockSpec((w.shape[0] // SIDE_CAST_STEPS, w.shape[1]), side_idx) for w in side]
    tile = lambda idx: pl.BlockSpec((PROJ_TM, PROJ_TN), idx)
    return pl.pallas_call(
        functools.partial(_proj_kernel, len(side)),
        grid=grid,
        in_specs=[pl.BlockSpec((PROJ_TM, D_MODEL), lambda i, j: (i, 0)),
                  pl.BlockSpec((D_MODEL, PROJ_TN), lambda i, j: (0, j)),
                  pl.BlockSpec((1, PROJ_TN), lambda i, j: (0, j))] + side_specs,
        out_specs=[tile(lambda i, j: (i, jnp.minimum(j, n_qkv))),
                   tile(lambda i, j: (i, jnp.maximum(j - n_qkv, 0)))] + side_specs,
        out_shape=[jax.ShapeDtypeStruct((m, D_QKV + PROJ_TN), F32),
                   jax.ShapeDtypeStruct((m, D_REST), BF16)]
                  + [jax.ShapeDtypeStruct(w.shape, BF16) for w in side],
        scratch_shapes=[pltpu.VMEM((PROJ_TM, D_MODEL), BF16)],
        compiler_params=_params(("arbitrary", "arbitrary")),
        name="in_proj",
    )(x2d, w_in, col_scale, *side)


def _rows(start, size, stride):
    return pl.ds(start, size) if stride == 1 else pl.ds(start, size, stride=stride)


def _attn_kernel(q_ref, k_ref, v_ref, bias_ref, o_ref, qf4, kf4, vf4, nsc, msc, dsc):
    win = 2 * BLOCK
    sub = SEQ // 4
    n_blk = sub // BLOCK

    def scores(refs, q_start, stride, first, pat, out_rows):
        qs, ks, vs = refs
        k_start = q_start if first else q_start - stride * BLOCK
        width = BLOCK if first else win
        col = pat * win + (BLOCK if first else 0)
        qb = qs[_rows(q_start, BLOCK, stride), :].astype(BF16)
        kw = ks[_rows(k_start, width, stride), :].astype(BF16)
        s = lax.dot_general(qb, kw, (((1,), (1,)), ((), ())), preferred_element_type=F32)
        s = s + bias_ref[0, :, col:col + width]
        m = jnp.max(s, axis=-1, keepdims=True)
        p = jnp.exp(s - m).astype(BF16)
        msc[pat, out_rows, :] = jnp.broadcast_to(m, (BLOCK, HEAD_DIM))
        return p, vs, _rows(k_start, width, stride), pat, out_rows

    def values(p, vs, k_rows, pat, out_rows):
        vw = vs[k_rows, :].astype(BF16)
        nd = jnp.dot(p, jnp.concatenate([vw, jnp.ones_like(vw)], axis=1),
                     preferred_element_type=F32)
        nsc[pat, out_rows, :] = nd[:, :HEAD_DIM]
        dsc[pat, out_rows, :] = nd[:, HEAD_DIM:]

    nat = (q_ref, k_ref, v_ref)
    mod4 = (qf4, kf4, vf4)
    blocks = []
    for n in range(SEQ // BLOCK):
        blocks.append((nat, n * BLOCK, 1, n == 0, 0, _rows(n * BLOCK, BLOCK, 1)))
    for c in range(4):
        for n in range(n_blk):
            blocks.append((nat, c + 4 * n * BLOCK, 4, n == 0, 1,
                           _rows(c * sub + n * BLOCK, BLOCK, 1)))
    for c in range(4):
        for r in range(4):
            start = c * sub + r
            blocks.append((mod4, start, 4, True, 2, _rows(start, BLOCK, 4)))

    for c in range(4):
        dst = _rows(c * sub, sub, 1)
        src = _rows(c, sub, 4)
        qf4[dst, :] = q_ref[src, :]
        kf4[dst, :] = k_ref[src, :]
        vf4[dst, :] = v_ref[src, :]

    pending = []
    for blk in blocks:
        pending.append(scores(*blk))
        if len(pending) > ATTN_PIPELINE_DEPTH:
            values(*pending.pop(0))
    for item in pending:
        values(*item)

    for c in range(4):
        for n in range(n_blk):
            tok = _rows(c + 4 * n * BLOCK, BLOCK, 4)
            m4 = _rows(c * sub + n * BLOCK, BLOCK, 1)
            m0, m1, m2 = msc[0, tok, :], msc[1, m4, :], msc[2, m4, :]
            top = jnp.maximum(jnp.maximum(m0, m1), m2)
            w0, w1, w2 = jnp.exp(m0 - top), jnp.exp(m1 - top), jnp.exp(m2 - top)
            num = w0 * nsc[0, tok, :] + w1 * nsc[1, m4, :] + w2 * nsc[2, m4, :]
            den = w0 * dsc[0, tok, :] + w1 * dsc[1, m4, :] + w2 * dsc[2, m4, :]
            o_ref[tok, :] = num / den


def _attention(qkv, bias_tiles, batch):
    width = bias_tiles.shape[-1]
    qkv_spec = lambda off: pl.BlockSpec((SEQ, HEAD_DIM), lambda b, h: (b, off + h))
    return pl.pallas_call(
        _attn_kernel,
        grid=(batch, N_HEADS_A),
        in_specs=[qkv_spec(0), qkv_spec(N_HEADS_A), qkv_spec(2 * N_HEADS_A),
                  pl.BlockSpec((1, BLOCK, width), lambda b, h: (h, 0, 0))],
        out_specs=pl.BlockSpec((SEQ, HEAD_DIM), lambda b, h: (b, h)),
        out_shape=jax.ShapeDtypeStruct((batch * SEQ, D_A), F32),
        scratch_shapes=[pltpu.VMEM((SEQ, HEAD_DIM), F32)] * 3
                       + [pltpu.VMEM((3, SEQ, HEAD_DIM), F32)] * 3,
        compiler_params=_params(("parallel", "arbitrary")),
        name="dilated_attn",
    )(qkv, qkv, qkv, bias_tiles)


def _merge_kernel(attn_ref, u_ref, vb_ref, ga_ref, gb_ref, x_ref, gv_ref, bv_ref,
                  ws_ref, bs_ref, wpa_ref, wpb_ref, wout_ref, g1_ref, b1_ref,
                  h_ref, gm):
    tm = attn_ref.shape[0]
    row = lax.broadcasted_iota(jnp.int32, (BLOCK, BLOCK), 0)
    col = lax.broadcasted_iota(jnp.int32, (BLOCK, BLOCK), 1)
    causal = col <= row
    y_a = jnp.dot(attn_ref[...].astype(BF16), wpa_ref[...], preferred_element_type=F32)
    vbn = _layer_norm(_gelu(vb_ref[...].astype(F32)), gv_ref[...], bv_ref[...]).astype(BF16)
    for g in range(N_GROUPS_B):
        ws = jnp.where(causal, ws_ref[g], 0.0).astype(BF16)
        cols = slice(g * BLOCK, (g + 1) * BLOCK)
        for c in range(0, tm // BLOCK, 2):
            rows = [slice((c + k) * BLOCK, (c + k + 1) * BLOCK) for k in range(2)]
            pair = jnp.concatenate([vbn[rows[0], cols], vbn[rows[1], cols]], axis=1)
            z = jnp.dot(ws, pair, preferred_element_type=F32) + bs_ref[g]
            for k in range(2):
                u = _gelu(u_ref[rows[k], cols].astype(F32))
                gm[rows[k], cols] = (u * z[:, k * BLOCK:(k + 1) * BLOCK]).astype(BF16)

    y_b = jnp.dot(gm[...], wpb_ref[...], preferred_element_type=F32)
    merged = (_sigmoid(ga_ref[...].astype(F32)) * y_a
              + _sigmoid(gb_ref[...].astype(F32)) * y_b).astype(BF16)
    mix = jnp.dot(merged, wout_ref[...], preferred_element_type=F32)
    h_ref[...] = _layer_norm(ALPHA * x_ref[...] + mix, g1_ref[...], b1_ref[...])


def _merge(attn, rest, x2d, ln_v_gain, ln_v_bias, w_spatial, b_spatial, w_proj_a, w_proj_b, w_out,
           ln1_gain, ln1_bias):
    m = x2d.shape[0]
    tm = MERGE_TM
    const = lambda shape: pl.BlockSpec(shape, lambda i: (0,) * len(shape),
                                       pipeline_mode=pl.Buffered(1))
    return pl.pallas_call(
        _merge_kernel,
        grid=(m // tm,),
        in_specs=[pl.BlockSpec((tm, D_A), lambda i: (i, 0)),
                  pl.BlockSpec((tm, D_B), lambda i: (i, 0)),
                  pl.BlockSpec((tm, D_B), lambda i: (i, 1)),
                  pl.BlockSpec((tm, D_MODEL), lambda i: (i, 1)),
                  pl.BlockSpec((tm, D_MODEL), lambda i: (i, 2)),
                  pl.BlockSpec((tm, D_MODEL), lambda i: (i, 0)),
                  const((1, D_B)), const((1, D_B)),
                  const((N_GROUPS_B, BLOCK, BLOCK)),
                  const((N_GROUPS_B, BLOCK, 1)),
                  const((D_A, D_MODEL)), const((D_B, D_MODEL)), const((D_MODEL, D_MODEL)),
                  const((1, D_MODEL)), const((1, D_MODEL))],
        out_specs=pl.BlockSpec((tm, D_MODEL), lambda i: (i, 0)),
        out_shape=jax.ShapeDtypeStruct((m, D_MODEL), F32),
        scratch_shapes=[pltpu.VMEM((tm, D_B), BF16)],
        compiler_params=_params(("parallel",)),
        name="merge_ln1",
    )(attn, rest, rest, rest, rest, x2d, ln_v_gain, ln_v_bias,
      w_spatial, b_spatial, w_proj_a, w_proj_b, w_out, ln1_gain, ln1_bias)


def _ffn_kernel(h_ref, w1_ref, b1_ref, w2_ref, b2_ref, g2_ref, be2_ref, o_ref, hb):
    f = pl.program_id(1)

    @pl.when(f == 0)
    def _():
        h = h_ref[...]
        hb[...] = h.astype(BF16)
        o_ref[...] = ALPHA * h + b2_ref[...]

    a = jnp.dot(hb[...], w1_ref[...], preferred_element_type=F32) + b1_ref[...]
    a = jnp.square(jnp.maximum(a, 0.0)).astype(BF16)
    o_ref[...] += jnp.dot(a, w2_ref[...], preferred_element_type=F32)

    @pl.when(f == pl.num_programs(1) - 1)
    def _():
        o_ref[...] = _layer_norm(o_ref[...], g2_ref[...], be2_ref[...])


def _ffn(h, w_ff1, b_ff1, w_ff2, b_ff2, ln2_gain, ln2_bias):
    m = h.shape[0]
    return pl.pallas_call(
        _ffn_kernel,
        grid=(m // FFN_TM, D_FF // FFN_TF),
        in_specs=[pl.BlockSpec((FFN_TM, D_MODEL), lambda i, f: (i, 0)),
                  pl.BlockSpec((D_MODEL, FFN_TF), lambda i, f: (0, f)),
                  pl.BlockSpec((1, FFN_TF), lambda i, f: (0, f)),
                  pl.BlockSpec((FFN_TF, D_MODEL), lambda i, f: (f, 0)),
                  pl.BlockSpec((1, D_MODEL), lambda i, f: (0, 0)),
                  pl.BlockSpec((1, D_MODEL), lambda i, f: (0, 0)),
                  pl.BlockSpec((1, D_MODEL), lambda i, f: (0, 0))],
        out_specs=pl.BlockSpec((FFN_TM, D_MODEL), lambda i, f: (i, 0)),
        out_shape=jax.ShapeDtypeStruct((m, D_MODEL), F32),
        scratch_shapes=[pltpu.VMEM((FFN_TM, D_MODEL), BF16)],
        compiler_params=_params(("parallel", "arbitrary")),
        name="ffn_ln2",
    )(h, w_ff1, b_ff1, w_ff2, b_ff2, ln2_gain, ln2_bias)


def kernel(x, w_in, rel_bias, ln_v_gain, ln_v_bias, w_spatial, b_spatial, w_proj_a, w_proj_b,
           w_out, ln1_gain, ln1_bias, w_ff1, b_ff1, w_ff2, b_ff2, ln2_gain, ln2_bias):
    batch, seq, d_model = x.shape
    assert (seq, d_model) == (SEQ, D_MODEL) and w_in.shape[0] == DEPTH
    bias_tiles = _bias_tiles(rel_bias)
    h = x.reshape(batch * seq, d_model)
    col_scale = jnp.concatenate([jnp.full((1, D_A), HEAD_DIM ** -0.5, F32),
                                 jnp.ones((1, D_QKV - D_A + D_REST), F32)], axis=1)
    for layer in range(DEPTH):
        w_in_b = w_in[layer].astype(BF16)
        qkv, rest, w_pa_b, w_pb_b, w_out_b, w_ff1_b, w_ff2_b = _project(
            h, w_in_b, col_scale,
            side=(w_proj_a[layer], w_proj_b[layer], w_out[layer], w_ff1[layer], w_ff2[layer]))
        attn = _attention(qkv, bias_tiles, batch)
        h = _merge(attn, rest, h, ln_v_gain[layer][None, :], ln_v_bias[layer][None, :],
                   w_spatial[layer], b_spatial[layer][:, :, None], w_pa_b, w_pb_b, w_out_b,
                   ln1_gain[layer][None, :], ln1_bias[layer][None, :])
        h = _ffn(h, w_ff1_b, b_ff1[layer][None, :], w_ff2_b, b_ff2[layer][None, :],
                 ln2_gain[layer][None, :], ln2_bias[layer][None, :])
    return h.reshape(batch, seq, d_model)
```
